```python
import math
import numpy as np
import jax
import jax.numpy as jnp
from jax import lax

D_MODEL = 2048
BATCH = 16
SEQ = 256
DEPTH = 4
DEC_BATCH = 4
DEC_SEQ = 1024
PAST_LEN = 512

GRID_W = 64
HEAD_DIM = 64
SSM_WIDTH = 768
SSM_GROUP = 16
SSM_GROUPS = SSM_WIDTH // SSM_GROUP
SSM_STATE = 64
WIN_HEADS = 12
WIN_KV_HEADS = 4
WIN_GROUP = WIN_HEADS // WIN_KV_HEADS
WINDOW = 128
WIN_BLOCK = 128
NA_HEADS = 12
NA_ROWS = 8
NA_COLS = 16
NA_QBLK = 16
NA_KBLK = 32
CTX_QBLK = 128
N_BRANCH = 3
BRANCH_W = 768
D_FF = 5632
CONV_W = 3
ROPE_BASE = 10000.0
EPS = 1e-6
NEG_INF = -1e30
ATT_SCALE = HEAD_DIM ** -0.5
WIN_Q = WIN_HEADS * HEAD_DIM
WIN_KV = WIN_KV_HEADS * HEAD_DIM
NA_W = NA_HEADS * HEAD_DIM
IN_SPLITS = (SSM_WIDTH, WIN_Q, WIN_KV, WIN_KV, NA_W, NA_W, NA_W, D_MODEL, D_MODEL, D_MODEL)
N_IN = sum(IN_SPLITS)

kernel_name = 'hybrid_diffusion_prefix_trunk_step'


def rmsnorm(x, g):
    xf = x.astype(jnp.float32)
    y = xf * lax.rsqrt(jnp.mean(xf * xf, axis=-1, keepdims=True) + EPS)
    return (y * g.astype(jnp.float32)).astype(x.dtype)


def modulate(h, shift, scale):
    return h * (1.0 + scale) + shift


def adaln(cond, w_mod, b_mod):
    m = jax.nn.silu(cond) @ w_mod + b_mod
    return jnp.split(m, 6, axis=-1)


def split_in(p):
    idx = [int(i) for i in np.cumsum(IN_SPLITS)[:-1]]
    return jnp.split(p, idx, axis=-1)


def heads(t, n):
    return t.reshape(t.shape[:-1] + (n, HEAD_DIM))


def rope_2d(x):
    L = x.shape[1]
    nf = HEAD_DIM // 4
    half = HEAD_DIM // 2
    pos = jnp.arange(L)
    row = (pos // GRID_W).astype(jnp.float32)
    col = (pos % GRID_W).astype(jnp.float32)
    inv = ROPE_BASE ** (-jnp.arange(nf, dtype=jnp.float32) / nf)
    xf = x.astype(jnp.float32)

    def rot(xh, p):
        ang = p[:, None] * inv[None, :]
        cos = jnp.cos(ang)[:, None, :]
        sin = jnp.sin(ang)[:, None, :]
        a, b = xh[..., :nf], xh[..., nf:]
        return jnp.concatenate([a * cos - b * sin, a * sin + b * cos], axis=-1)

    out = jnp.concatenate([rot(xf[..., :half], row), rot(xf[..., half:], col)], axis=-1)
    return out.astype(x.dtype)


def _lin_recur(e1, e2):
    a1, b1 = e1
    a2, b2 = e2
    return a1 * a2, a2 * b1 + b2


def s5_bidir(u, lam_re, lam_im, log_dt, b_re, b_im, c_re, c_im, d_skip, h0_re, h0_im):
    f32 = jnp.float32
    uf = u.astype(f32)
    uc = uf.astype(jnp.complex64)
    lam = lax.complex(lam_re.astype(f32), lam_im.astype(f32))
    dt = jnp.exp(log_dt.astype(f32))[..., None]
    lam_bar = jnp.exp(lam * dt)
    b_bar = ((lam_bar - 1.0) / lam)[..., None] * lax.complex(b_re.astype(f32), b_im.astype(f32))
    c_mat = lax.complex(c_re.astype(f32), c_im.astype(f32))
    h0 = None if h0_re is None else lax.complex(h0_re.astype(f32), h0_im.astype(f32))

    def scan_dir(i, reverse):
        bu = jnp.einsum('blgn,gpn->blgp', uc, b_bar[i])
        if h0 is not None:
            bu = bu.at[:, -1 if reverse else 0].add(lam_bar[i] * h0[:, i])
        a = jnp.broadcast_to(lam_bar[i], bu.shape)
        _, xs = lax.associative_scan(_lin_recur, (a, bu), reverse=reverse, axis=1)
        return xs

    xf = scan_dir(0, False)
    xb = scan_dir(1, True)
    y = jnp.real(jnp.einsum('blgp,gnp->blgn', xf, c_mat[0]) + jnp.einsum('blgp,gnp->blgn', xb, c_mat[1]))
    y = y + d_skip.astype(f32) * uf
    return y.astype(u.dtype), xf, xb


def ssm_mixer(u, lam_re, lam_im, log_dt, b_re, b_im, c_re, c_im, d_skip, w_glu, b_glu, h0_re, h0_im):
    B, L, _ = u.shape
    y, xf, xb = s5_bidir(u.reshape(B, L, SSM_GROUPS, SSM_GROUP), lam_re, lam_im, log_dt,
                         b_re, b_im, c_re, c_im, d_skip, h0_re, h0_im)
    y = jax.nn.gelu(y.reshape(B, L, SSM_WIDTH))
    return y * jax.nn.sigmoid(y @ w_glu + b_glu), xf, xb


def ctx_self_attention(q, k, v, sink):
    B, L, HK, G, d = q.shape
    nb = L // CTX_QBLK
    qb = jnp.moveaxis(q.reshape(B, nb, CTX_QBLK, HK, G, d), 1, 0)

    def one(qi):
        s = jnp.einsum('bqhgd,bkhd->bhgqk', qi, k).astype(jnp.float32) * ATT_SCALE
        if sink is None:
            p = jax.nn.softmax(s, axis=-1)
        else:
            col = jnp.broadcast_to(sink.astype(jnp.float32)[None, :, :, None, None], s.shape[:-1] + (1,))
            p = jax.nn.softmax(jnp.concatenate([s, col], axis=-1), axis=-1)[..., :-1]
        return jnp.einsum('bhgqk,bkhd->bqhgd', p.astype(v.dtype), v)

    o = lax.map(one, qb)
    return jnp.moveaxis(o, 0, 1).reshape(B, L, HK * G * d)


def window_attention_latent(q, k, v, kc, vc, sink):
    B, L, HK, G, d = q.shape
    Lc = kc.shape[1]
    nb = L // WIN_BLOCK
    qb = q.reshape(B, nb, WIN_BLOCK, HK, G, d)

    def bands(t):
        tp = jnp.pad(t, ((0, 0), (WIN_BLOCK, WIN_BLOCK), (0, 0), (0, 0)))
        return jnp.concatenate(
            [tp[:, i * WIN_BLOCK:i * WIN_BLOCK + L].reshape(B, nb, WIN_BLOCK, HK, d) for i in range(3)], axis=2)

    kb, vb = bands(k), bands(v)
    qpos = np.arange(nb)[:, None] * WIN_BLOCK + np.arange(WIN_BLOCK)[None, :]
    kpos = np.arange(nb)[:, None] * WIN_BLOCK - WIN_BLOCK + np.arange(3 * WIN_BLOCK)[None, :]
    valid = ((np.abs(qpos[:, :, None] - kpos[:, None, :]) <= WINDOW)
             & (kpos[:, None, :] >= 0) & (kpos[:, None, :] < L))
    s_loc = jnp.einsum('bnqhgd,bnkhd->bnhgqk', qb, kb).astype(jnp.float32) * ATT_SCALE
    s_loc = jnp.where(valid[None, :, None, None], s_loc, NEG_INF)
    s_ctx = jnp.einsum('bnqhgd,bchd->bnhgqc', qb, kc).astype(jnp.float32) * ATT_SCALE
    col = jnp.broadcast_to(sink.astype(jnp.float32)[None, None, :, :, None, None], s_loc.shape[:-1] + (1,))
    p = jax.nn.softmax(jnp.concatenate([s_loc, s_ctx, col], axis=-1), axis=-1)
    nl = 3 * WIN_BLOCK
    p_loc = p[..., :nl].astype(v.dtype)
    p_ctx = p[..., nl:nl + Lc].astype(v.dtype)
    o = jnp.einsum('bnhgqk,bnkhd->bnqhgd', p_loc, vb) + jnp.einsum('bnhgqc,bchd->bnqhgd', p_ctx, vc)
    return o.reshape(B, L, HK * G * d)


def na_latent(q, k, v, kc, vc, rpb):
    B, L, H, d = q.shape
    R = L // GRID_W
    WR = min(NA_ROWS, R)
    NCB = GRID_W // NA_QBLK
    r = np.arange(R)
    row_start = np.clip(r - WR // 2, 0, R - WR)
    row_idx = row_start[:, None] + np.arange(WR)[None, :]
    j = np.arange(NCB)
    cb_start = np.clip(j * NA_QBLK - NA_COLS // 2, 0, GRID_W - NA_KBLK)
    col_idx = cb_start[:, None] + np.arange(NA_KBLK)[None, :]
    qcol = j[:, None] * NA_QBLK + np.arange(NA_QBLK)[None, :]
    qcol_start = np.clip(qcol - NA_COLS // 2, 0, GRID_W - NA_COLS)
    kcol = col_idx[:, None, :]
    col_valid = (kcol >= qcol_start[:, :, None]) & (kcol < qcol_start[:, :, None] + NA_COLS)
    dr_idx = row_idx - r[:, None] + NA_ROWS - 1
    dc_idx = np.clip(kcol - qcol[:, :, None] + NA_COLS - 1, 0, 2 * NA_COLS - 2)

    krb = k.reshape(B, R, GRID_W, H, d)[:, row_idx][:, :, :, col_idx]
    vrb = v.reshape(B, R, GRID_W, H, d)[:, row_idx][:, :, :, col_idx]
    qg = q.reshape(B, R, NCB, NA_QBLK, H, d)
    s_loc = jnp.einsum('brjqhd,brwjkhd->brjhqwk', qg, krb).astype(jnp.float32) * ATT_SCALE
    bias = rpb.astype(jnp.float32)[:, dr_idx[:, None, None, :, None], dc_idx[None, :, :, None, :]]
    bias = jnp.transpose(bias, (1, 2, 0, 3, 4, 5))
    s_loc = jnp.where(col_valid[None, None, :, None, :, None, :], s_loc + bias[None], NEG_INF)
    nl = WR * NA_KBLK
    s_loc = s_loc.reshape(B, R, NCB, H, NA_QBLK, nl)
    s_ctx = jnp.einsum('brjqhd,bchd->brjhqc', qg, kc).astype(jnp.float32) * ATT_SCALE
    p = jax.nn.softmax(jnp.concatenate([s_loc, s_ctx], axis=-1), axis=-1)
    p_loc = p[..., :nl].reshape(B, R, NCB, H, NA_QBLK, WR, NA_KBLK).astype(v.dtype)
    p_ctx = p[..., nl:].astype(v.dtype)
    o = jnp.einsum('brjhqwk,brwjkhd->brjqhd', p_loc, vrb) + jnp.einsum('brjhqc,bchd->brjqhd', p_ctx, vc)
    return o.reshape(B, L, H * d)


def merge_branches(o_ssm, o_win, o_na, ga, gb, gc, w_branch, w_out):
    m = (jax.nn.sigmoid(ga) * (o_ssm @ w_branch[0])
         + jax.nn.sigmoid(gb) * (o_win @ w_branch[1])
         + jax.nn.sigmoid(gc) * (o_na @ w_branch[2]))
    return m @ w_out


def conv_ffn(h, w_up, conv_w, conv_b, w_down):
    L = h.shape[1]
    u = h @ w_up
    pad = CONV_W // 2
    up = jnp.pad(u, ((0, 0), (pad, pad), (0, 0)))
    u = sum(up[:, i:i + L] * conv_w[i] for i in range(CONV_W)) + conv_b
    a, b = jnp.split(u, 2, axis=-1)
    return (jax.nn.silu(a) * b) @ w_down


def setup_inputs(seed: int = 0) -> dict:
    key = jax.random.key(seed)
    ks = iter(jax.random.split(key, 48))
    f32 = jnp.float32

    def nrm(shape, scale):
        return jax.random.normal(next(ks), shape, f32) * scale

    n = jnp.arange(SSM_STATE, dtype=f32)
    return {
        'x_prompt': nrm((BATCH, SEQ, D_MODEL), 1.0),
        'x_sample': nrm((DEC_BATCH, DEC_SEQ, D_MODEL), 1.0),
        'cache_win_k': nrm((DEC_BATCH, DEPTH, PAST_LEN, WIN_KV_HEADS, HEAD_DIM), 1.0),
        'cache_win_v': nrm((DEC_BATCH, DEPTH, PAST_LEN, WIN_KV_HEADS, HEAD_DIM), 1.0),
        'cache_na_k': nrm((DEC_BATCH, DEPTH, PAST_LEN, NA_HEADS, HEAD_DIM), 1.0),
        'cache_na_v': nrm((DEC_BATCH, DEPTH, PAST_LEN, NA_HEADS, HEAD_DIM), 1.0),
        'state_ssm_re': nrm((DEC_BATCH, DEPTH, 2, SSM_GROUPS, SSM_STATE), 0.1),
        'state_ssm_im': nrm((DEC_BATCH, DEPTH, 2, SSM_GROUPS, SSM_STATE), 0.1),
        'c': nrm((DEC_BATCH, D_MODEL), 1.0),
        'c_ctx': nrm((D_MODEL,), 1.0),
        'norm1_g': 1.0 + nrm((DEPTH, D_MODEL), 0.02),
        'norm2_g': 1.0 + nrm((DEPTH, D_MODEL), 0.02),
        'w_mod': nrm((DEPTH, D_MODEL, 6 * D_MODEL), D_MODEL ** -0.5),
        'b_mod': nrm((DEPTH, 6 * D_MODEL), 0.01),
        'w_in': nrm((DEPTH, D_MODEL, N_IN), D_MODEL ** -0.5),
        'ssm_lam_re': -0.5 + nrm((DEPTH, 2, SSM_GROUPS, SSM_STATE), 0.01),
        'ssm_lam_im': jnp.pi * n + nrm((DEPTH, 2, SSM_GROUPS, SSM_STATE), 0.01),
        'ssm_log_dt': jax.random.uniform(next(ks), (DEPTH, 2, SSM_GROUPS), f32, math.log(1e-3), math.log(1e-1)),
        'ssm_b_re': nrm((DEPTH, 2, SSM_GROUPS, SSM_STATE, SSM_GROUP), (2 * SSM_GROUP) ** -0.5),
        'ssm_b_im': nrm((DEPTH, 2, SSM_GROUPS, SSM_STATE, SSM_GROUP), (2 * SSM_GROUP) ** -0.5),
        'ssm_c_re': nrm((DEPTH, 2, SSM_GROUPS, SSM_GROUP, SSM_STATE), SSM_STATE ** -0.5),
        'ssm_c_im': nrm((DEPTH, 2, SSM_GROUPS, SSM_GROUP, SSM_STATE), SSM_STATE ** -0.5),
        'ssm_d': nrm((DEPTH, SSM_GROUPS, SSM_GROUP), 1.0),
        'w_glu': nrm((DEPTH, SSM_WIDTH, SSM_WIDTH), SSM_WIDTH ** -0.5),
        'b_glu': nrm((DEPTH, SSM_WIDTH), 0.01),
        'win_sink': nrm((DEPTH, WIN_HEADS), 0.5),
        'na_rpb': nrm((DEPTH, NA_HEADS, 2 * NA_ROWS - 1, 2 * NA_COLS - 1), 0.1),
        'w_branch': nrm((DEPTH, N_BRANCH, BRANCH_W, D_MODEL), BRANCH_W ** -0.5),
        'w_out': nrm((DEPTH, D_MODEL, D_MODEL), D_MODEL ** -0.5),
        'w_up': nrm((DEPTH, D_MODEL, 2 * D_FF), D_MODEL ** -0.5),
        'conv_w': nrm((DEPTH, CONV_W, 2 * D_FF), 0.5),
        'conv_b': nrm((DEPTH, 2 * D_FF), 0.01),
        'w_down': nrm((DEPTH, D_FF, D_MODEL), D_FF ** -0.5),
        'final_g': 1.0 + nrm((D_MODEL,), 0.02),
    }


def reference(x_prompt, x_sample, cache_win_k, cache_win_v, cache_na_k, cache_na_v, state_ssm_re, state_ssm_im,
              c, c_ctx, norm1_g, norm2_g, w_mod, b_mod, w_in, ssm_lam_re, ssm_lam_im, ssm_log_dt,
              ssm_b_re, ssm_b_im, ssm_c_re, ssm_c_im, ssm_d, w_glu, b_glu, win_sink, na_rpb,
              w_branch, w_out, w_up, conv_w, conv_b, w_down, final_g):
    xp, xs = x_prompt, x_sample
    Bp, Lp, _ = xp.shape
    Bs, Ls, _ = xs.shape
    new_wk, new_wv, new_nk, new_nv, new_sre, new_sim = [], [], [], [], [], []
    for l in range(DEPTH):
        ssm_p = (ssm_lam_re[l], ssm_lam_im[l], ssm_log_dt[l], ssm_b_re[l], ssm_b_im[l],
                 ssm_c_re[l], ssm_c_im[l], ssm_d[l], w_glu[l], b_glu[l])
        sink = win_sink[l].reshape(WIN_KV_HEADS, WIN_GROUP)

        sh1, sc1, g1, sh2, sc2, g2 = adaln(c_ctx, w_mod[l], b_mod[l])
        h = modulate(rmsnorm(xp, norm1_g[l]), sh1, sc1)
        u, qw, kw, vw, qn, kn, vn, ga, gb, gc = split_in(h @ w_in[l])
        kw, vw = heads(kw, WIN_KV_HEADS), heads(vw, WIN_KV_HEADS)
        kn, vn = heads(kn, NA_HEADS), heads(vn, NA_HEADS)
        o_ssm, xf, xb = ssm_mixer(u, *ssm_p, None, None)
        fin = jnp.stack([xf[:, -1], xb[:, 0]], axis=1)
        o_win = ctx_self_attention(qw.reshape(Bp, Lp, WIN_KV_HEADS, WIN_GROUP, HEAD_DIM), kw, vw, sink)
        o_na = ctx_self_attention(qn.reshape(Bp, Lp, NA_HEADS, 1, HEAD_DIM), kn, vn, None)
        xp = xp + g1 * merge_branches(o_ssm, o_win, o_na, ga, gb, gc, w_branch[l], w_out[l])
        h = modulate(rmsnorm(xp, norm2_g[l]), sh2, sc2)
        xp = xp + g2 * conv_ffn(h, w_up[l], conv_w[l], conv_b[l], w_down[l])
        new_wk.append(kw)
        new_wv.append(vw)
        new_nk.append(kn)
        new_nv.append(vn)
        new_sre.append(jnp.real(fin))
        new_sim.append(jnp.imag(fin))

        sh1, sc1, g1, sh2, sc2, g2 = [t[:, None, :] for t in adaln(c, w_mod[l], b_mod[l])]
        h = modulate(rmsnorm(xs, norm1_g[l]), sh1, sc1)
        u, qw, kw, vw, qn, kn, vn, ga, gb, gc = split_in(h @ w_in[l])
        o_ssm, _, _ = ssm_mixer(u, *ssm_p, state_ssm_re[:, l], state_ssm_im[:, l])
        qw = rope_2d(heads(qw, WIN_HEADS)).reshape(Bs, Ls, WIN_KV_HEADS, WIN_GROUP, HEAD_DIM)
        kw = rope_2d(heads(kw, WIN_KV_HEADS))
        o_win = window_attention_latent(qw, kw, heads(vw, WIN_KV_HEADS), cache_win_k[:, l], cache_win_v[:, l], sink)
        o_na = na_latent(heads(qn, NA_HEADS), heads(kn, NA_HEADS), heads(vn, NA_HEADS),
                         cache_na_k[:, l], cache_na_v[:, l], na_rpb[l])
        xs = xs + g1 * merge_branches(o_ssm, o_win, o_na, ga, gb, gc, w_branch[l], w_out[l])
        h = modulate(rmsnorm(xs, norm2_g[l]), sh2, sc2)
        xs = xs + g2 * conv_ffn(h, w_up[l], conv_w[l], conv_b[l], w_down[l])

    y_prompt = rmsnorm(xp, final_g)
    y_sample = rmsnorm(xs, final_g)
    new_win_k = jnp.stack(new_wk, axis=1)
    new_win_v = jnp.stack(new_wv, axis=1)
    new_na_k = jnp.stack(new_nk, axis=1)
    new_na_v = jnp.stack(new_nv, axis=1)
    new_ssm_re = jnp.stack(new_sre, axis=1)
    new_ssm_im = jnp.stack(new_sim, axis=1)
    return (y_prompt, y_sample, new_win_k, new_win_v, new_na_k, new_na_v, new_ssm_re, new_ssm_im)
```

```python
import functools

import numpy as np
import jax
import jax.numpy as jnp
from jax import lax
from jax.experimental import pallas as pl
from jax.experimental.pallas import tpu as pltpu

F32 = jnp.float32
BF16 = jnp.bfloat16

D_MODEL = 2048
BATCH = 16
SEQ = 256
DEPTH = 4
DEC_BATCH = 4
DEC_SEQ = 1024
PAST_LEN = 512
GRID_W = 64
GRID_H = DEC_SEQ // GRID_W
HEAD_DIM = 64
SSM_WIDTH = 768
SSM_GROUP = 16
SSM_GROUPS = SSM_WIDTH // SSM_GROUP
SSM_STATE = 64
WIN_HEADS = 12
WIN_KV_HEADS = 4
WIN_GROUP = WIN_HEADS // WIN_KV_HEADS
WINDOW = 128
WIN_BLOCK = 128
NA_HEADS = 12
NA_ROWS = 8
NA_COLS = 16
BRANCH_W = 768
D_FF = 5632
CONV_W = 3
ROPE_BASE = 10000.0
EPS = 1e-6
NEG_INF = -1e30
ATT_SCALE = HEAD_DIM ** -0.5
WIN_KV = WIN_KV_HEADS * HEAD_DIM

N_CTX = BATCH * SEQ
N_LAT = DEC_BATCH * DEC_SEQ
N_TOK = N_CTX + N_LAT

LANES = 128
SUBLANES = 8
HALF = LANES // 2
MIB = 1024 * 1024

U_OFF = 0
QW_OFF = 768
QN_OFF = 1536
KN_OFF = 2304
VN_OFF = 3072
KW_OFF = 3840
VW_OFF = 4096
N_PROJ = 4352
N_GATE = 3 * D_MODEL

SSM_KB = SSM_WIDTH // LANES
GROUPS_PER_KB = LANES // SSM_GROUP
STATE_COLS = GROUPS_PER_KB * SSM_STATE
SCAN_T = 64
SCAN_ROWS = SCAN_T * SUBLANES

NA_QROWS = 2
NA_QBLOCKS = GRID_H // NA_QROWS
NA_WIN_ROWS = 10
NA_WIN_KEYS = NA_WIN_ROWS * GRID_W
RPB_H = 2 * NA_ROWS - 1
RPB_W = 2 * NA_COLS - 1


def _cparams(dims, vmem_mib):
    return pltpu.CompilerParams(dimension_semantics=dims, vmem_limit_bytes=int(vmem_mib * MIB))


def _mod_row(i, tm):
    tiles_ctx = N_CTX // tm
    per_batch = DEC_SEQ // tm
    return jnp.where(i < tiles_ctx, 0, 1 + (i - tiles_ctx) // per_batch)


def _mod_spec(layer, which, tm, tn, col_of):
    return pl.BlockSpec((None, None, None, 1, tn),
                        lambda i, j: (layer, which, _mod_row(i, tm), 0, col_of(i, j)))


def _adaln_kernel(c_ref, w_ref, b_ref, o_ref):
    s = jax.nn.silu(c_ref[...]).astype(BF16)
    o_ref[...] = jnp.dot(s, w_ref[...], preferred_element_type=F32) + b_ref[...]


def _adaln(cond8, w_mod, b_mod):
    tn = 1024
    n = 6 * D_MODEL
    return pl.pallas_call(
        _adaln_kernel,
        grid=(DEPTH, n // tn),
        in_specs=[pl.BlockSpec((SUBLANES, D_MODEL), lambda l, j: (0, 0)),
                  pl.BlockSpec((None, D_MODEL, tn), lambda l, j: (l, 0, j)),
                  pl.BlockSpec((None, 1, tn), lambda l, j: (l, 0, j))],
        out_specs=pl.BlockSpec((None, SUBLANES, tn), lambda l, j: (l, 0, j)),
        out_shape=jax.ShapeDtypeStruct((DEPTH, SUBLANES, n), F32),
        compiler_params=_cparams(("parallel", "parallel"), 24),
        name="adaln",
    )(cond8, w_mod, b_mod.reshape(DEPTH, 1, n))


def _norm_mod(x, g, sh, sc):
    y = x * lax.rsqrt(jnp.mean(x * x, axis=-1, keepdims=True) + EPS)
    y = y * g
    return y * (1.0 + sc) + sh


def _norm_mod_matmul_kernel(x_ref, g_ref, sh_ref, sc_ref, w_ref, o_ref, h_ref):
    @pl.when(pl.program_id(1) == 0)
    def _():
        h_ref[...] = _norm_mod(x_ref[...], g_ref[...], sh_ref[...], sc_ref[...]).astype(BF16)

    o_ref[...] = jnp.dot(h_ref[...], w_ref[...], preferred_element_type=F32)


def _norm_mod_matmul(x, norm_g, mods, w, layer, tm, tn):
    n = w.shape[-1]
    zero = lambda i, j: 0
    return pl.pallas_call(
        _norm_mod_matmul_kernel,
        grid=(N_TOK // tm, n // tn),
        in_specs=[pl.BlockSpec((tm, D_MODEL), lambda i, j: (i, 0)),
                  pl.BlockSpec((None, 1, D_MODEL), lambda i, j: (layer, 0, 0)),
                  _mod_spec(layer, 0, tm, D_MODEL, zero),
                  _mod_spec(layer, 1, tm, D_MODEL, zero),
                  pl.BlockSpec((None, D_MODEL, tn), lambda i, j: (layer, 0, j))],
        out_specs=pl.BlockSpec((tm, tn), lambda i, j: (i, j)),
        out_shape=jax.ShapeDtypeStruct((N_TOK, n), F32),
        scratch_shapes=[pltpu.VMEM((tm, D_MODEL), BF16)],
        compiler_params=_cparams(("parallel", "arbitrary"), 52),
        name="norm_mod_matmul",
    )(x, norm_g.reshape(DEPTH, 1, D_MODEL), mods, mods, w)


def _matmul_residual_kernel(a_ref, w_ref, x_ref, g_ref, o_ref):
    acc = jnp.dot(a_ref[...], w_ref[...], preferred_element_type=F32)
    o_ref[...] = x_ref[...] + g_ref[...] * acc


def _matmul_residual(a, w, x, mods, layer, which, tm, tn):
    k = a.shape[-1]
    return pl.pallas_call(
        _matmul_residual_kernel,
        grid=(N_TOK // tm, D_MODEL // tn),
        in_specs=[pl.BlockSpec((tm, k), lambda i, j: (i, 0)),
                  pl.BlockSpec((None, k, tn), lambda i, j: (layer, 0, j)),
                  pl.BlockSpec((tm, tn), lambda i, j: (i, j)),
                  _mod_spec(layer, which, tm, tn, lambda i, j: j)],
        out_specs=pl.BlockSpec((tm, tn), lambda i, j: (i, j)),
        out_shape=jax.ShapeDtypeStruct((N_TOK, D_MODEL), F32),
        compiler_params=_cparams(("parallel", "arbitrary"), 52),
        name="matmul_residual",
    )(a, w, x, mods)


def _branch_merge_kernel(o0_ref, o1_ref, o2_ref, w0_ref, w1_ref, w2_ref, g0_ref, g1_ref, g2_ref, m_ref):
    m = jax.nn.sigmoid(g0_ref[...]) * jnp.dot(o0_ref[...], w0_ref[...], preferred_element_type=F32)
    m = m + jax.nn.sigmoid(g1_ref[...]) * jnp.dot(o1_ref[...], w1_ref[...], preferred_element_type=F32)
    m = m + jax.nn.sigmoid(g2_ref[...]) * jnp.dot(o2_ref[...], w2_ref[...], preferred_element_type=F32)
    m_ref[...] = m.astype(BF16)


def _branch_merge(o_ssm, o_win, o_na, w_branch, gates, layer):
    tm, tn = 1024, 512
    nj = D_MODEL // tn
    o_spec = pl.BlockSpec((tm, BRANCH_W), lambda i, j: (i, 0))
    w_spec = lambda k: pl.BlockSpec((None, None, BRANCH_W, tn), lambda i, j: (layer, k, 0, j))
    g_spec = lambda k: pl.BlockSpec((tm, tn), lambda i, j: (i, k * nj + j))
    return pl.pallas_call(
        _branch_merge_kernel,
        grid=(N_TOK // tm, nj),
        in_specs=[o_spec, o_spec, o_spec, w_spec(0), w_spec(1), w_spec(2), g_spec(0), g_spec(1), g_spec(2)],
        out_specs=pl.BlockSpec((tm, tn), lambda i, j: (i, j)),
        out_shape=jax.ShapeDtypeStruct((N_TOK, D_MODEL), BF16),
        compiler_params=_cparams(("parallel", "arbitrary"), 48),
        name="branch_merge",
    )(o_ssm, o_win, o_na, w_branch, w_branch, w_branch, gates, gates, gates)


FFN_TM = 1024
FFN_TF = 512


def _ffn_up_kernel(x_ref, g_ref, sh_ref, sc_ref, wa_ref, wb_ref, cwa_ref, cwb_ref, cba_ref, cbb_ref,
                   o_ref, h_ref):
    i = pl.program_id(0)

    @pl.when(pl.program_id(1) == 0)
    def _():
        h_ref[...] = _norm_mod(x_ref[...], g_ref[...], sh_ref[...], sc_ref[...]).astype(BF16)

    seq = jnp.where(i < N_CTX // FFN_TM, SEQ, DEC_SEQ)
    pos = lax.broadcasted_iota(jnp.int32, (FFN_TM, 1), 0) & (seq - 1)
    first = pos == 0
    last = pos == seq - 1

    def conv(u, cw_ref, cb_ref):
        prev = jnp.where(first, 0.0, pltpu.roll(u, 1, 0))
        nxt = jnp.where(last, 0.0, pltpu.roll(u, FFN_TM - 1, 0))
        return prev * cw_ref[0:1, :] + u * cw_ref[1:2, :] + nxt * cw_ref[2:3, :] + cb_ref[...]

    h = h_ref[...]
    a = conv(jnp.dot(h, wa_ref[...], preferred_element_type=F32), cwa_ref, cba_ref)
    b = conv(jnp.dot(h, wb_ref[...], preferred_element_type=F32), cwb_ref, cbb_ref)
    o_ref[...] = (jax.nn.silu(a) * b).astype(BF16)


def _ffn_up(x, norm_g, mods, w_up, conv_w, conv_b, layer):
    tm, tf = FFN_TM, FFN_TF
    nj = D_FF // tf
    zero = lambda i, j: 0
    conv_b3 = conv_b.reshape(DEPTH, 1, 2 * D_FF)
    return pl.pallas_call(
        _ffn_up_kernel,
        grid=(N_TOK // tm, nj),
        in_specs=[pl.BlockSpec((tm, D_MODEL), lambda i, j: (i, 0)),
                  pl.BlockSpec((None, 1, D_MODEL), lambda i, j: (layer, 0, 0)),
                  _mod_spec(layer, 3, tm, D_MODEL, zero),
                  _mod_spec(layer, 4, tm, D_MODEL, zero),
                  pl.BlockSpec((None, D_MODEL, tf), lambda i, j: (layer, 0, j)),
                  pl.BlockSpec((None, D_MODEL, tf), lambda i, j: (layer, 0, nj + j)),
                  pl.BlockSpec((None, CONV_W, tf), lambda i, j: (layer, 0, j)),
                  pl.BlockSpec((None, CONV_W, tf), lambda i, j: (layer, 0, nj + j)),
                  pl.BlockSpec((None, 1, tf), lambda i, j: (layer, 0, j)),
                  pl.BlockSpec((None, 1, tf), lambda i, j: (layer, 0, nj + j))],
        out_specs=pl.BlockSpec((tm, tf), lambda i, j: (i, j)),
        out_shape=jax.ShapeDtypeStruct((N_TOK, D_FF), BF16),
        scratch_shapes=[pltpu.VMEM((tm, D_MODEL), BF16)],
        compiler_params=_cparams(("parallel", "arbitrary"), 56),
        name="ffn_up",
    )(x, norm_g.reshape(DEPTH, 1, D_MODEL), mods, mods, w_up, w_up, conv_w, conv_w, conv_b3, conv_b3)


def _rmsnorm_kernel(x_ref, g_ref, o_ref):
    x = x_ref[...]
    o_ref[...] = x * lax.rsqrt(jnp.mean(x * x, axis=-1, keepdims=True) + EPS) * g_ref[...]


def _final_norm(x, g):
    tm = 512
    return pl.pallas_call(
        _rmsnorm_kernel,
        grid=(N_TOK // tm,),
        in_specs=[pl.BlockSpec((tm, D_MODEL), lambda i: (i, 0)),
                  pl.BlockSpec((1, D_MODEL), lambda i: (0, 0))],
        out_specs=pl.BlockSpec((tm, D_MODEL), lambda i: (i, 0)),
        out_shape=jax.ShapeDtypeStruct((N_TOK, D_MODEL), F32),
        compiler_params=_cparams(("parallel",), 32),
        name="final_norm",
    )(x, g.reshape(1, D_MODEL))


def _dot_nt(a, b):
    return lax.dot_general(a, b, (((1,), (1,)), ((), ())), preferred_element_type=F32)


def _lane_is_low(rows):
    return lax.broadcasted_iota(jnp.int32, (rows, LANES), 1) < HALF


def _one_head_query(q2, low, half, kv_half):
    qa = jnp.where(low if half == 0 else jnp.logical_not(low), q2, 0.0)
    if half != kv_half:
        qa = pltpu.roll(qa, HALF, 1)
    return qa.astype(BF16)


def _ctx_attn_kernel(sink_ref, qw_ref, kw_ref, vw_ref, qn_ref, kn_ref, vn_ref, ow_ref, on_ref, *, layer):
    low = _lane_is_low(SEQ)

    def attend(qa, k2, v2, sink):
        s = _dot_nt(qa, k2)
        m = jnp.max(s, axis=-1, keepdims=True)
        if sink is not None:
            m = jnp.maximum(m, sink)
        e = jnp.exp(s - m)
        den = jnp.sum(e, axis=-1, keepdims=True)
        if sink is not None:
            den = den + jnp.exp(sink - m)
        return jnp.dot(e.astype(BF16), v2, preferred_element_type=F32) / den

    for j in range(WIN_HEADS // 2):
        q2 = qw_ref[:, j * LANES:(j + 1) * LANES] * ATT_SCALE
        outs = []
        for half in range(2):
            head = 2 * j + half
            kv = head // WIN_GROUP
            kt, kv_half = kv // 2, kv % 2
            k2 = kw_ref[:, kt * LANES:(kt + 1) * LANES].astype(BF16)
            v2 = vw_ref[:, kt * LANES:(kt + 1) * LANES].astype(BF16)
            o = attend(_one_head_query(q2, low, half, kv_half), k2, v2, sink_ref[layer, head])
            if half != kv_half:
                o = pltpu.roll(o, HALF, 1)
            outs.append(o)
        ow_ref[:, j * LANES:(j + 1) * LANES] = jnp.where(low, outs[0], outs[1]).astype(BF16)

    for j in range(NA_HEADS // 2):
        q2 = qn_ref[:, j * LANES:(j + 1) * LANES] * ATT_SCALE
        k2 = kn_ref[:, j * LANES:(j + 1) * LANES].astype(BF16)
        v2 = vn_ref[:, j * LANES:(j + 1) * LANES].astype(BF16)
        outs = [attend(_one_head_query(q2, low, half, half), k2, v2, None) for half in range(2)]
        on_ref[:, j * LANES:(j + 1) * LANES] = jnp.where(low, outs[0], outs[1]).astype(BF16)


def _ctx_attention(proj, win_sink, layer):
    w768 = lambda off: pl.BlockSpec((SEQ, 768), lambda b: (b, off // 768))
    w256 = lambda off: pl.BlockSpec((SEQ, WIN_KV), lambda b: (b, off // WIN_KV))
    out_spec = pl.BlockSpec((SEQ, 768), lambda b: (b, 0))
    return pl.pallas_call(
        functools.partial(_ctx_attn_kernel, layer=layer),
        grid=(BATCH,),
        in_specs=[pl.BlockSpec(memory_space=pltpu.SMEM),
                  w768(QW_OFF), w256(KW_OFF), w256(VW_OFF), w768(QN_OFF), w768(KN_OFF), w768(VN_OFF)],
        out_specs=[out_spec, out_spec],
        out_shape=[jax.ShapeDtypeStruct((N_CTX, 768), BF16)] * 2,
        compiler_params=_cparams(("parallel",), 32),
        name="ctx_attention",
    )(win_sink, proj, proj, proj, proj, proj, proj)


def _rope_tables():
    nf = HEAD_DIM // 4
    pos = np.arange(DEC_SEQ)
    row = (pos // GRID_W).astype(np.float32)
    col = (pos % GRID_W).astype(np.float32)
    inv = (np.float32(ROPE_BASE) ** (-np.arange(nf, dtype=np.float32) / np.float32(nf))).astype(np.float32)
    ang_row = (row[:, None] * inv[None, :]).astype(np.float32)
    ang_col = (col[:, None] * inv[None, :]).astype(np.float32)
    cos_h = np.concatenate([np.cos(ang_row), np.cos(ang_row), np.cos(ang_col), np.cos(ang_col)], axis=-1)
    sin_h = np.concatenate([-np.sin(ang_row), np.sin(ang_row), -np.sin(ang_col), np.sin(ang_col)], axis=-1)
    cos_q = np.tile(cos_h, (1, WIN_HEADS)).astype(np.float32)
    sin_q = np.tile(sin_h, (1, WIN_HEADS)).astype(np.float32)
    return cos_q, sin_q


def _rope(x, cos, sin_signed):
    width = x.shape[-1]
    q = HEAD_DIM // 4
    lane = lax.broadcasted_iota(jnp.int32, x.shape, 1)
    partner = jnp.where((lane & (2 * q - 1)) < q, pltpu.roll(x, width - q, 1), pltpu.roll(x, q, 1))
    return x * cos + partner * sin_signed


def _win_attn_kernel(sink_ref, q_ref, kp_ref, kc_ref, kn_ref, vp_ref, vc_ref, vn_ref, ck_ref, cv_ref,
                     cosq_ref, sinq_ref, cosp_ref, sinp_ref, cosc_ref, sinc_ref, cosn_ref, sinn_ref,
                     o_ref, *, layer):
    n = pl.program_id(1)
    nb = DEC_SEQ // WIN_BLOCK
    low = _lane_is_low(WIN_BLOCK)

    q = _rope(q_ref[...], cosq_ref[...], sinq_ref[...]) * ATT_SCALE
    k_loc = jnp.concatenate([_rope(kp_ref[...], cosp_ref[...], sinp_ref[...]),
                             _rope(kc_ref[...], cosc_ref[...], sinc_ref[...]),
                             _rope(kn_ref[...], cosn_ref[...], sinn_ref[...])], axis=0).astype(BF16)
    v_loc = jnp.concatenate([vp_ref[...], vc_ref[...], vn_ref[...]], axis=0).astype(BF16)
    k_ctx = ck_ref[...].astype(BF16)
    v_ctx = cv_ref[...].astype(BF16)

    qi = lax.broadcasted_iota(jnp.int32, (WIN_BLOCK, 3 * WIN_BLOCK), 0)
    kj = lax.broadcasted_iota(jnp.int32, (WIN_BLOCK, 3 * WIN_BLOCK), 1)
    rel = kj - WIN_BLOCK - qi
    valid = jnp.logical_and(rel >= -WINDOW, rel <= WINDOW)
    valid = jnp.logical_and(valid, kj >= jnp.where(n > 0, 0, WIN_BLOCK))
    valid = jnp.logical_and(valid, kj < jnp.where(n < nb - 1, 3 * WIN_BLOCK, 2 * WIN_BLOCK))

    for j in range(WIN_HEADS // 2):
        q2 = q[:, j * LANES:(j + 1) * LANES]
        outs = []
        for half in range(2):
            head = 2 * j + half
            kv = head // WIN_GROUP
            kt, kv_half = kv // 2, kv % 2
            cols = slice(kt * LANES, (kt + 1) * LANES)
            qa = _one_head_query(q2, low, half, kv_half)
            s_loc = jnp.where(valid, _dot_nt(qa, k_loc[:, cols]), NEG_INF)
            s_ctx = _dot_nt(qa, k_ctx[:, cols])
            sink = sink_ref[layer, head]
            m = jnp.maximum(jnp.max(s_loc, axis=-1, keepdims=True), jnp.max(s_ctx, axis=-1, keepdims=True))
            m = jnp.maximum(m, sink)
            e_loc = jnp.exp(s_loc - m)
            e_ctx = jnp.exp(s_ctx - m)
            den = (jnp.sum(e_loc, axis=-1, keepdims=True) + jnp.sum(e_ctx, axis=-1, keepdims=True)
                   + jnp.exp(sink - m))
            o = (jnp.dot(e_loc.astype(BF16), v_loc[:, cols], preferred_element_type=F32)
                 + jnp.dot(e_ctx.astype(BF16), v_ctx[:, cols], preferred_element_type=F32)) / den
            if half != kv_half:
                o = pltpu.roll(o, HALF, 1)
            outs.append(o)
        o_ref[:, j * LANES:(j + 1) * LANES] = jnp.where(low, outs[0], outs[1]).astype(BF16)


def _win_attention(proj, cache_k, cache_v, win_sink, tables, layer):
    nb = DEC_SEQ // WIN_BLOCK
    row0 = N_CTX // WIN_BLOCK
    cos_q, sin_q, cos_k, sin_k = tables
    prev = lambda n: jnp.maximum(n - 1, 0)
    cur = lambda n: n
    nxt = lambda n: jnp.minimum(n + 1, nb - 1)
    kv_spec = lambda off, f: pl.BlockSpec((WIN_BLOCK, WIN_KV), lambda b, n: (row0 + b * nb + f(n), off // WIN_KV))
    tab_q = pl.BlockSpec((WIN_BLOCK, 768), lambda b, n: (n, 0))
    tab_k = lambda f: pl.BlockSpec((WIN_BLOCK, WIN_KV), lambda b, n: (f(n), 0))
    cache_spec = pl.BlockSpec((None, None, PAST_LEN, WIN_KV), lambda b, n: (b, layer, 0, 0))
    return pl.pallas_call(
        functools.partial(_win_attn_kernel, layer=layer),
        grid=(DEC_BATCH, nb),
        in_specs=[pl.BlockSpec(memory_space=pltpu.SMEM),
                  pl.BlockSpec((WIN_BLOCK, 768), lambda b, n: (row0 + b * nb + n, QW_OFF // 768)),
                  kv_spec(KW_OFF, prev), kv_spec(KW_OFF, cur), kv_spec(KW_OFF, nxt),
                  kv_spec(VW_OFF, prev), kv_spec(VW_OFF, cur), kv_spec(VW_OFF, nxt),
                  cache_spec, cache_spec,
                  tab_q, tab_q, tab_k(prev), tab_k(prev), tab_k(cur), tab_k(cur), tab_k(nxt), tab_k(nxt)],
        out_specs=pl.BlockSpec((WIN_BLOCK, 768), lambda b, n: (b * nb + n, 0)),
        out_shape=jax.ShapeDtypeStruct((N_LAT, 768), BF16),
        compiler_params=_cparams(("parallel", "parallel"), 32),
        name="win_attention",
    )(win_sink, proj, proj, proj, proj, proj, proj, proj, cache_k, cache_v,
      cos_q, sin_q, cos_k, sin_k, cos_k, sin_k, cos_k, sin_k)


def _na_window_start(r0):
    return min(max(r0 - NA_ROWS // 2, 0), GRID_H - NA_WIN_ROWS)


def _na_bias_kernel(rpb_ref, o_ref):
    head = pl.program_id(0)
    base = head * (RPB_H * RPB_W)
    qc = lax.broadcasted_iota(jnp.int32, (GRID_W, LANES), 0)
    lane = lax.broadcasted_iota(jnp.int32, (GRID_W, LANES), 1)
    kc = lane & (GRID_W - 1)
    low = lane < HALF
    dc_idx = jnp.clip(kc - qc + NA_COLS - 1, 0, RPB_W - 1)
    q_start = jnp.clip(qc - NA_COLS // 2, 0, GRID_W - NA_COLS)
    col_valid = jnp.logical_and(kc >= q_start, kc < q_start + NA_COLS)
    neg = jnp.full((GRID_W, LANES), NEG_INF, F32)

    pair = {}
    for d in range(-NA_ROWS, NA_ROWS):
        acc = jnp.zeros((GRID_W, LANES), F32)
        for dc in range(RPB_W):
            s_lo = rpb_ref[base + (d + NA_ROWS - 1) * RPB_W + dc] if abs(d) < NA_ROWS else 0.0
            s_hi = rpb_ref[base + (d + NA_ROWS) * RPB_W + dc] if abs(d + 1) < NA_ROWS else 0.0
            acc = jnp.where(dc_idx == dc, jnp.where(low, s_lo, s_hi), acc)
        pair[d] = jnp.where(col_valid, acc, NEG_INF)

    for qb in range(NA_QBLOCKS):
        r0 = NA_QROWS * qb
        ws = _na_window_start(r0)
        for qi in range(NA_QROWS):
            qr = r0 + qi
            rs = min(max(qr - NA_ROWS // 2, 0), GRID_H - NA_ROWS)
            for p in range(NA_WIN_ROWS // 2):
                kr = ws + 2 * p
                ok_lo = rs <= kr < rs + NA_ROWS
                ok_hi = rs <= kr + 1 < rs + NA_ROWS
                d = kr - qr
                if not (ok_lo or ok_hi):
                    tile = neg
                else:
                    tile = pair[d]
                    if not ok_lo:
                        tile = jnp.where(low, NEG_INF, tile)
                    if not ok_hi:
                        tile = jnp.where(low, tile, NEG_INF)
                o_ref[qb, qi * GRID_W:(qi + 1) * GRID_W, p * LANES:(p + 1) * LANES] = tile


def _na_bias(rpb_l):
    return pl.pallas_call(
        _na_bias_kernel,
        grid=(NA_HEADS,),
        in_specs=[pl.BlockSpec(memory_space=pltpu.SMEM)],
        out_specs=pl.BlockSpec((None, NA_QBLOCKS, NA_QROWS * GRID_W, NA_WIN_KEYS), lambda h: (h, 0, 0, 0)),
        out_shape=jax.ShapeDtypeStruct((NA_HEADS, NA_QBLOCKS, NA_QROWS * GRID_W, NA_WIN_KEYS), F32),
        compiler_params=_cparams(("parallel",), 32),
        name="na_bias",
    )(rpb_l.reshape(NA_HEADS * RPB_H * RPB_W))


def _na_attn_kernel(q_ref, k_ref, v_ref, ck_ref, cv_ref, bias_ref, o_ref):
    qb = pl.program_id(2)
    nq = NA_QROWS * GRID_W
    low = _lane_is_low(nq)
    ws = jnp.clip(NA_QROWS * qb - NA_ROWS // 2, 0, GRID_H - NA_WIN_ROWS)
    start = pl.multiple_of(ws * GRID_W, LANES)
    k_loc = k_ref[pl.ds(start, NA_WIN_KEYS), :].astype(BF16)
    v_loc = v_ref[pl.ds(start, NA_WIN_KEYS), :].astype(BF16)
    k_ctx = ck_ref[...].astype(BF16)
    v_ctx = cv_ref[...].astype(BF16)
    q2 = q_ref[...] * ATT_SCALE
    outs = []
    for half in range(2):
        qa = _one_head_query(q2, low, half, half)
        s_loc = _dot_nt(qa, k_loc) + bias_ref[half]
        s_ctx = _dot_nt(qa, k_ctx)
        m = jnp.maximum(jnp.max(s_loc, axis=-1, keepdims=True), jnp.max(s_ctx, axis=-1, keepdims=True))
        e_loc = jnp.exp(s_loc - m)
        e_ctx = jnp.exp(s_ctx - m)
        den = jnp.sum(e_loc, axis=-1, keepdims=True) + jnp.sum(e_ctx, axis=-1, keepdims=True)
        outs.append((jnp.dot(e_loc.astype(BF16), v_loc, preferred_element_type=F32)
                     + jnp.dot(e_ctx.astype(BF16), v_ctx, preferred_element_type=F32)) / den)
    o_ref[...] = jnp.where(low, outs[0], outs[1]).astype(BF16)


def _na_attention(proj, cache_k, cache_v, bias, layer):
    nq = NA_QROWS * GRID_W
    seq_blk0 = N_CTX // DEC_SEQ
    q_blk0 = N_CTX // nq
    kv_spec = lambda off: pl.BlockSpec((DEC_SEQ, LANES), lambda hp, b, qb: (seq_blk0 + b, off // LANES + hp))
    cache_spec = pl.BlockSpec((None, None, PAST_LEN, LANES), lambda hp, b, qb: (b, layer, 0, hp))
    return pl.pallas_call(
        _na_attn_kernel,
        grid=(NA_HEADS // 2, DEC_BATCH, NA_QBLOCKS),
        in_specs=[pl.BlockSpec((nq, LANES), lambda hp, b, qb: (q_blk0 + b * NA_QBLOCKS + qb, QN_OFF // LANES + hp)),
                  kv_spec(KN_OFF), kv_spec(VN_OFF), cache_spec, cache_spec,
                  pl.BlockSpec((2, None, nq, NA_WIN_KEYS), lambda hp, b, qb: (hp, qb, 0, 0))],
        out_specs=pl.BlockSpec((nq, LANES), lambda hp, b, qb: (b * NA_QBLOCKS + qb, hp)),
        out_shape=jax.ShapeDtypeStruct((N_LAT, 768), BF16),
        compiler_params=_cparams(("parallel", "parallel", "parallel"), 32),
        name="na_attention",
    )(proj, proj, proj, cache_k, cache_v, bias)


def _ssm_prep_kernel(lr_ref, li_ref, ldt_ref, br_ref, bi_ref, lbr_ref, lbi_ref, bbr_ref, bbi_ref):
    lr, li = lr_ref[...], li_ref[...]
    dt = jnp.exp(ldt_ref[...])
    mag = jnp.exp(lr * dt)
    ang = li * dt
    lbr = mag * jnp.cos(ang)
    lbi = mag * jnp.sin(ang)
    nr = lbr - 1.0
    den = lr * lr + li * li
    cr = (nr * lr + lbi * li) / den
    ci = (lbi * lr - nr * li) / den
    br, bi = br_ref[...], bi_ref[...]
    lbr_ref[...] = lbr
    lbi_ref[...] = lbi
    bbr_ref[...] = cr * br - ci * bi
    bbi_ref[...] = cr * bi + ci * br


def _ssm_prep(lam_re, lam_im, log_dt, b_re, b_im):
    shape5 = (DEPTH, 2, SSM_GROUPS, SSM_GROUP, SSM_STATE)
    rows = DEPTH * 2 * SSM_GROUPS * SSM_GROUP
    bc = lambda t: jnp.broadcast_to(t, shape5).reshape(rows, SSM_STATE)
    args = (bc(lam_re[:, :, :, None, :]), bc(lam_im[:, :, :, None, :]), bc(log_dt[:, :, :, None, None]),
            jnp.swapaxes(b_re, -1, -2).reshape(rows, SSM_STATE), jnp.swapaxes(b_im, -1, -2).reshape(rows, SSM_STATE))
    outs = pl.pallas_call(
        _ssm_prep_kernel,
        out_shape=[jax.ShapeDtypeStruct((rows, SSM_STATE), F32)] * 4,
        compiler_params=_cparams(None, 32),
        name="ssm_prep",
    )(*args)
    lbr, lbi, bbr, bbi = [o.reshape(shape5) for o in outs]
    return lbr[:, :, :, 0], lbi[:, :, :, 0], bbr, bbi


def _ssm_scan_kernel(u_ref, h0_ref, lam_ref, bf_ref, bb_ref, cf_ref, cb_ref, y_ref, fin_ref, st_ref, bu_ref):
    @pl.when(pl.program_id(1) == 0)
    def _():
        st_ref[...] = h0_ref[...]

    fwd = (lax.broadcasted_iota(jnp.int32, (SCAN_ROWS, 1), 0) & (SUBLANES - 1)) < SUBLANES // 2
    for kb in range(SSM_KB):
        ub = u_ref[:, kb * LANES:(kb + 1) * LANES].astype(BF16)
        bu_ref[...] = jnp.where(fwd, jnp.dot(ub, bf_ref[kb], preferred_element_type=F32),
                                jnp.dot(ub, bb_ref[kb], preferred_element_type=F32))
        ar = lam_ref[kb, 0]
        ai = lam_ref[kb, 1]

        def step(t, carry):
            xr, xi = carry
            r = pl.multiple_of(t * SUBLANES, SUBLANES)
            nr = ar * xr - ai * xi + bu_ref[pl.ds(r, SUBLANES), 0:STATE_COLS]
            ni = ar * xi + ai * xr + bu_ref[pl.ds(r, SUBLANES), STATE_COLS:2 * STATE_COLS]
            bu_ref[pl.ds(r, SUBLANES), 0:STATE_COLS] = nr
            bu_ref[pl.ds(r, SUBLANES), STATE_COLS:2 * STATE_COLS] = ni
            return nr, ni

        xr, xi = lax.fori_loop(0, SCAN_T, step, (st_ref[kb, 0], st_ref[kb, 1]))
        st_ref[kb, 0] = xr
        st_ref[kb, 1] = xi
        xs = bu_ref[...].astype(BF16)
        y_ref[:, kb * LANES:(kb + 1) * LANES] = jnp.where(
            fwd, jnp.dot(xs, cf_ref[kb], preferred_element_type=F32),
            jnp.dot(xs, cb_ref[kb], preferred_element_type=F32))
    fin_ref[...] = st_ref[...]


def _ssm_scan(u_dual, h0, lam, bf, bb, cf, cb):
    nbb, rows, _ = u_dual.shape
    state_shape = (SSM_KB, 2, SUBLANES, STATE_COLS)
    whole = lambda shape: pl.BlockSpec(shape, lambda b, c: (0,) * len(shape))
    state_spec = pl.BlockSpec((None,) + state_shape, lambda b, c: (b, 0, 0, 0, 0))
    return pl.pallas_call(
        _ssm_scan_kernel,
        grid=(nbb, rows // SCAN_ROWS),
        in_specs=[pl.BlockSpec((None, SCAN_ROWS, SSM_WIDTH), lambda b, c: (b, c, 0)),
                  state_spec, whole(state_shape),
                  whole((SSM_KB, LANES, 2 * STATE_COLS)), whole((SSM_KB, LANES, 2 * STATE_COLS)),
                  whole((SSM_KB, 2 * STATE_COLS, LANES)), whole((SSM_KB, 2 * STATE_COLS, LANES))],
        out_specs=[pl.BlockSpec((None, SCAN_ROWS, SSM_WIDTH), lambda b, c: (b, c, 0)), state_spec],
        out_shape=[jax.ShapeDtypeStruct((nbb, rows, SSM_WIDTH), F32),
                   jax.ShapeDtypeStruct((nbb,) + state_shape, F32)],
        scratch_shapes=[pltpu.VMEM(state_shape, F32), pltpu.VMEM((SCAN_ROWS, 2 * STATE_COLS), F32)],
        compiler_params=_cparams(("parallel", "arbitrary"), 32),
        name="ssm_scan",
    )(u_dual, h0, lam, bf, bb, cf, cb)


def _ssm_layer_params(lbr, lbi, bbr, bbi, c_re, c_im):
    eye = jnp.eye(GROUPS_PER_KB, dtype=F32)

    def lam_tile(t):
        t = t.reshape(2, SSM_KB, STATE_COLS).transpose(1, 0, 2)
        return jnp.repeat(t, SUBLANES // 2, axis=1)

    def b_blocks(t):
        t = t.reshape(SSM_KB, GROUPS_PER_KB, SSM_GROUP, SSM_STATE)
        return jnp.einsum('kgnp,gh->kgnhp', t, eye).reshape(SSM_KB, LANES, STATE_COLS)

    def c_blocks(t):
        t = t.reshape(SSM_KB, GROUPS_PER_KB, SSM_GROUP, SSM_STATE)
        return jnp.einsum('kgnp,gh->kgphn', t, eye).reshape(SSM_KB, STATE_COLS, LANES)

    lam = jnp.stack([lam_tile(lbr), lam_tile(lbi)], axis=1)
    bmat = [jnp.concatenate([b_blocks(bbr[d]), b_blocks(bbi[d])], axis=-1).astype(BF16) for d in range(2)]
    cmat = [jnp.concatenate([c_blocks(c_re[d]), -c_blocks(c_im[d])], axis=1).astype(BF16) for d in range(2)]
    return lam, bmat[0], bmat[1], cmat[0], cmat[1]


def _to_dual(u, length):
    nbb = u.shape[0] // 4
    f = u.reshape(nbb, 4, length, SSM_WIDTH).transpose(0, 2, 1, 3)
    return jnp.concatenate([f, f[:, ::-1]], axis=2).reshape(nbb, length * SUBLANES, SSM_WIDTH)


def _from_dual(y, length):
    nbb = y.shape[0]
    y = y.reshape(nbb, length, SUBLANES, SSM_WIDTH)
    yf = y[:, :, :4].transpose(0, 2, 1, 3).reshape(nbb * 4 * length, SSM_WIDTH)
    yb = y[:, ::-1, 4:].transpose(0, 2, 1, 3).reshape(nbb * 4 * length, SSM_WIDTH)
    return yf, yb


def _ssm_glu_kernel(yf_ref, yb_ref, u_ref, d_ref, w_ref, b_ref, o_ref):
    y = yf_ref[...] + yb_ref[...] + d_ref[...] * u_ref[...]
    y = jax.nn.gelu(y)
    z = jnp.dot(y.astype(BF16), w_ref[...], preferred_element_type=F32) + b_ref[...]
    o_ref[...] = (y * jax.nn.sigmoid(z)).astype(BF16)


def _ssm_glu(yf, yb, proj, d_skip, w_glu, b_glu, layer):
    tm = 512
    row = pl.BlockSpec((tm, SSM_WIDTH), lambda i: (i, 0))
    vec = pl.BlockSpec((None, 1, SSM_WIDTH), lambda i: (layer, 0, 0))
    return pl.pallas_call(
        _ssm_glu_kernel,
        grid=(N_TOK // tm,),
        in_specs=[row, row, pl.BlockSpec((tm, SSM_WIDTH), lambda i: (i, U_OFF // SSM_WIDTH)), vec,
                  pl.BlockSpec((None, SSM_WIDTH, SSM_WIDTH), lambda i: (layer, 0, 0)), vec],
        out_specs=row,
        out_shape=jax.ShapeDtypeStruct((N_TOK, SSM_WIDTH), BF16),
        compiler_params=_cparams(("parallel",), 32),
        name="ssm_glu",
    )(yf, yb, proj, d_skip.reshape(DEPTH, 1, SSM_WIDTH), w_glu, b_glu.reshape(DEPTH, 1, SSM_WIDTH))


def _ssm_mixer(proj, scan_params, h0_lat, d_skip, w_glu, b_glu, layer):
    u = proj[:, U_OFF:U_OFF + SSM_WIDTH]
    u_ctx = _to_dual(u[:N_CTX].reshape(BATCH, SEQ, SSM_WIDTH), SEQ)
    u_lat = _to_dual(u[N_CTX:].reshape(DEC_BATCH, DEC_SEQ, SSM_WIDTH), DEC_SEQ)
    h0_ctx = jnp.zeros((BATCH // 4, SSM_KB, 2, SUBLANES, STATE_COLS), F32)
    y_ctx, fin = _ssm_scan(u_ctx, h0_ctx, *scan_params)
    y_lat, _ = _ssm_scan(u_lat, h0_lat, *scan_params)
    yf_c, yb_c = _from_dual(y_ctx, SEQ)
    yf_l, yb_l = _from_dual(y_lat, DEC_SEQ)
    yf = jnp.concatenate([yf_c, yf_l], axis=0)
    yb = jnp.concatenate([yb_c, yb_l], axis=0)
    return _ssm_glu(yf, yb, proj, d_skip, w_glu, b_glu, layer), fin


def _state_to_tiles(s_re, s_im):
    def one(t):
        t = t.transpose(1, 0, 2, 3).reshape(SUBLANES, SSM_KB, STATE_COLS)
        return t.transpose(1, 0, 2)
    return jnp.stack([one(s_re), one(s_im)], axis=1)[None]


def _tiles_to_state(fin, part):
    t = fin[:, :, part].reshape(BATCH // 4, SSM_KB, 2, 4, GROUPS_PER_KB, SSM_STATE)
    return t.transpose(0, 3, 2, 1, 4, 5).reshape(BATCH, 2, SSM_GROUPS, SSM_STATE)


def kernel(x_prompt, x_sample, cache_win_k, cache_win_v, cache_na_k, cache_na_v, state_ssm_re, state_ssm_im,
           c, c_ctx, norm1_g, norm2_g, w_mod, b_mod, w_in, ssm_lam_re, ssm_lam_im, ssm_log_dt,
           ssm_b_re, ssm_b_im, ssm_c_re, ssm_c_im, ssm_d, w_glu, b_glu, win_sink, na_rpb,
           w_branch, w_out, w_up, conv_w, conv_b, w_down, final_g):
    x = jnp.concatenate([x_prompt.reshape(N_CTX, D_MODEL), x_sample.reshape(N_LAT, D_MODEL)], axis=0)

    w_proj = jnp.concatenate([w_in[:, :, 0:1536], w_in[:, :, 2048:4352], w_in[:, :, 1536:2048]], axis=-1).astype(BF16)
    w_gate = w_in[:, :, 4352:].astype(BF16)
    w_mod_b, w_glu_b, w_branch_b = w_mod.astype(BF16), w_glu.astype(BF16), w_branch.astype(BF16)
    w_out_b, w_up_b, w_down_b = w_out.astype(BF16), w_up.astype(BF16), w_down.astype(BF16)

    cond8 = jnp.concatenate([c_ctx[None], c, jnp.zeros((SUBLANES - 1 - DEC_BATCH, D_MODEL), F32)], axis=0)
    mods = _adaln(cond8, w_mod_b, b_mod)
    mods = mods.reshape(DEPTH, SUBLANES, 6, D_MODEL).transpose(0, 2, 1, 3)[:, :, :, None, :]

    lbr, lbi, bbr, bbi = _ssm_prep(ssm_lam_re, ssm_lam_im, ssm_log_dt, ssm_b_re, ssm_b_im)
    c_re_t, c_im_t = ssm_c_re, ssm_c_im

    cos_q, sin_q = _rope_tables()
    tables = (jnp.asarray(cos_q), jnp.asarray(sin_q), jnp.asarray(cos_q[:, :WIN_KV]), jnp.asarray(sin_q[:, :WIN_KV]))
    cwk = cache_win_k.reshape(DEC_BATCH, DEPTH, PAST_LEN, WIN_KV)
    cwv = cache_win_v.reshape(DEC_BATCH, DEPTH, PAST_LEN, WIN_KV)
    cnk = cache_na_k.reshape(DEC_BATCH, DEPTH, PAST_LEN, NA_HEADS * HEAD_DIM)
    cnv = cache_na_v.reshape(DEC_BATCH, DEPTH, PAST_LEN, NA_HEADS * HEAD_DIM)

    new_wk, new_wv, new_nk, new_nv, new_sre, new_sim = [], [], [], [], [], []
    for l in range(DEPTH):
        proj = _norm_mod_matmul(x, norm1_g, mods, w_proj, l, tm=512, tn=2176)
        gates = _norm_mod_matmul(x, norm1_g, mods, w_gate, l, tm=1024, tn=1024)

        scan_params = _ssm_layer_params(lbr[l], lbi[l], bbr[l], bbi[l], c_re_t[l], c_im_t[l])
        h0_lat = _state_to_tiles(state_ssm_re[:, l], state_ssm_im[:, l])
        o_ssm, fin = _ssm_mixer(proj, scan_params, h0_lat, ssm_d, w_glu_b, b_glu, l)

        o_win_c, o_na_c = _ctx_attention(proj, win_sink, l)
        o_win_l = _win_attention(proj, cwk, cwv, win_sink, tables, l)
        o_na_l = _na_attention(proj, cnk, cnv, _na_bias(na_rpb[l]), l)
        o_win = jnp.concatenate([o_win_c, o_win_l], axis=0)
        o_na = jnp.concatenate([o_na_c, o_na_l], axis=0)

        merged = _branch_merge(o_ssm, o_win, o_na, w_branch_b, gates, l)
        x = _matmul_residual(merged, w_out_b, x, mods, l, 2, tm=1024, tn=512)
        act = _ffn_up(x, norm2_g, mods, w_up_b, conv_w, conv_b, l)
        x = _matmul_residual(act, w_down_b, x, mods, l, 5, tm=512, tn=512)

        ctx = proj[:N_CTX]
        new_wk.append(ctx[:, KW_OFF:KW_OFF + WIN_KV].reshape(BATCH, SEQ, WIN_KV_HEADS, HEAD_DIM))
        new_wv.append(ctx[:, VW_OFF:VW_OFF + WIN_KV].reshape(BATCH, SEQ, WIN_KV_HEADS, HEAD_DIM))
        new_nk.append(ctx[:, KN_OFF:KN_OFF + 768].reshape(BATCH, SEQ, NA_HEADS, HEAD_DIM))
        new_nv.append(ctx[:, VN_OFF:VN_OFF + 768].reshape(BATCH, SEQ, NA_HEADS, HEAD_DIM))
        new_sre.append(_tiles_to_state(fin, 0))
        new_sim.append(_tiles_to_state(fin, 1))

    y = _final_norm(x, final_g)
    return (y[:N_CTX].reshape(BATCH, SEQ, D_MODEL), y[N_CTX:].reshape(DEC_BATCH, DEC_SEQ, D_MODEL),
            jnp.stack(new_wk, axis=1), jnp.stack(new_wv, axis=1), jnp.stack(new_nk, axis=1),
            jnp.stack(new_nv, axis=1), jnp.stack(new_sre, axis=1), jnp.stack(new_sim, axis=1))
```

```python
import functools

import numpy as np
import jax
import jax.numpy as jnp
from jax import lax
from jax.experimental import pallas as pl
from jax.experimental.pallas import tpu as pltpu

F32 = jnp.float32
BF16 = jnp.bfloat16

D_MODEL = 2048
BATCH = 16
SEQ = 256
DEPTH = 4
DEC_BATCH = 4
DEC_SEQ = 1024
PAST_LEN = 512
GRID_W = 64
GRID_H = DEC_SEQ // GRID_W
HEAD_DIM = 64
SSM_WIDTH = 768
SSM_GROUP = 16
SSM_GROUPS = SSM_WIDTH // SSM_GROUP
SSM_STATE = 64
WIN_HEADS = 12
WIN_KV_HEADS = 4
WIN_GROUP = WIN_HEADS // WIN_KV_HEADS
WINDOW = 128
WIN_BLOCK = 128
NA_HEADS = 12
NA_ROWS = 8
NA_COLS = 16
BRANCH_W = 768
D_FF = 5632
CONV_W = 3
ROPE_BASE = 10000.0
EPS = 1e-6
NEG_INF = -1e30
ATT_SCALE = HEAD_DIM ** -0.5
WIN_KV = WIN_KV_HEADS * HEAD_DIM

N_CTX = BATCH * SEQ
N_LAT = DEC_BATCH * DEC_SEQ
N_TOK = N_CTX + N_LAT

LANES = 128
SUBLANES = 8
HALF = LANES // 2
MIB = 1024 * 1024

GATE_OFF = 0
U_OFF = 3 * D_MODEL
QW_OFF = U_OFF + 768
QN_OFF = QW_OFF + 768
KN_OFF = QN_OFF + 768
VN_OFF = KN_OFF + 768
KW_OFF = VN_OFF + 768
VW_OFF = KW_OFF + WIN_KV
N_IN = VW_OFF + WIN_KV
IN_TN = 1536
N_PROJ = -(-N_IN // IN_TN) * IN_TN

SSM_KB = SSM_WIDTH // LANES
GROUPS_PER_KB = LANES // SSM_GROUP
STATE_COLS = GROUPS_PER_KB * SSM_STATE
SCAN_T = 64
SCAN_ROWS = SCAN_T * SUBLANES

NA_QROWS = 2
NA_QBLOCKS = GRID_H // NA_QROWS
NA_WIN_ROWS = 10
NA_WIN_KEYS = NA_WIN_ROWS * GRID_W
RPB_H = 2 * NA_ROWS - 1
RPB_W = 2 * NA_COLS - 1


def _cparams(dims, vmem_mib):
    return pltpu.CompilerParams(dimension_semantics=dims, vmem_limit_bytes=int(vmem_mib * MIB))


def _mod_row(i, tm):
    tiles_ctx = N_CTX // tm
    per_batch = DEC_SEQ // tm
    return jnp.where(i < tiles_ctx, 0, 1 + (i - tiles_ctx) // per_batch)


def _mod_spec(layer, which, tm, tn, col_of):
    return pl.BlockSpec((None, None, None, 1, tn),
                        lambda i, j: (layer, which, _mod_row(i, tm), 0, col_of(i, j)))


def _adaln_kernel(c_ref, w_ref, b_ref, o_ref):
    s = jax.nn.silu(c_ref[...]).astype(BF16)
    o_ref[...] = jnp.dot(s, w_ref[...], preferred_element_type=F32) + b_ref[...]


def _adaln(cond8, w_mod, b_mod):
    tn = 1024
    n = 6 * D_MODEL
    return pl.pallas_call(
        _adaln_kernel,
        grid=(DEPTH, n // tn),
        in_specs=[pl.BlockSpec((SUBLANES, D_MODEL), lambda l, j: (0, 0)),
                  pl.BlockSpec((None, D_MODEL, tn), lambda l, j: (l, 0, j)),
                  pl.BlockSpec((None, 1, tn), lambda l, j: (l, 0, j))],
        out_specs=pl.BlockSpec((None, SUBLANES, tn), lambda l, j: (l, 0, j)),
        out_shape=jax.ShapeDtypeStruct((DEPTH, SUBLANES, n), F32),
        compiler_params=_cparams(("parallel", "parallel"), 24),
        name="adaln",
    )(cond8, w_mod, b_mod.reshape(DEPTH, 1, n))


def _norm_mod(x, g, sh, sc):
    y = x * lax.rsqrt(jnp.mean(x * x, axis=-1, keepdims=True) + EPS)
    y = y * g
    return y * (1.0 + sc) + sh


def _norm_mod_matmul_kernel(x_ref, g_ref, sh_ref, sc_ref, w_ref, o_ref, h_ref):
    @pl.when(pl.program_id(1) == 0)
    def _():
        h_ref[...] = _norm_mod(x_ref[...], g_ref[...], sh_ref[...], sc_ref[...]).astype(BF16)

    o_ref[...] = jnp.dot(h_ref[...], w_ref[...], preferred_element_type=F32)


def _norm_mod_matmul(x, norm_g, mods, w, layer, tm, tn):
    n = w.shape[-1]
    zero = lambda i, j: 0
    return pl.pallas_call(
        _norm_mod_matmul_kernel,
        grid=(N_TOK // tm, n // tn),
        in_specs=[pl.BlockSpec((tm, D_MODEL), lambda i, j: (i, 0)),
                  pl.BlockSpec((None, 1, D_MODEL), lambda i, j: (layer, 0, 0)),
                  _mod_spec(layer, 0, tm, D_MODEL, zero),
                  _mod_spec(layer, 1, tm, D_MODEL, zero),
                  pl.BlockSpec((None, D_MODEL, tn), lambda i, j: (layer, 0, j))],
        out_specs=pl.BlockSpec((tm, tn), lambda i, j: (i, j)),
        out_shape=jax.ShapeDtypeStruct((N_TOK, n), F32),
        scratch_shapes=[pltpu.VMEM((tm, D_MODEL), BF16)],
        compiler_params=_cparams(("parallel", "arbitrary"), 56),
        name="norm_mod_matmul",
    )(x, norm_g.reshape(DEPTH, 1, D_MODEL), mods, mods, w)


def _matmul_residual_kernel(a_ref, w_ref, x_ref, g_ref, o_ref):
    acc = jnp.dot(a_ref[...], w_ref[...], preferred_element_type=F32)
    o_ref[...] = x_ref[...] + g_ref[...] * acc


def _matmul_residual(a, w, x, mods, layer, which, tm, tn):
    k = a.shape[-1]
    return pl.pallas_call(
        _matmul_residual_kernel,
        grid=(N_TOK // tm, D_MODEL // tn),
        in_specs=[pl.BlockSpec((tm, k), lambda i, j: (i, 0)),
                  pl.BlockSpec((None, k, tn), lambda i, j: (layer, 0, j)),
                  pl.BlockSpec((tm, tn), lambda i, j: (i, j)),
                  _mod_spec(layer, which, tm, tn, lambda i, j: j)],
        out_specs=pl.BlockSpec((tm, tn), lambda i, j: (i, j)),
        out_shape=jax.ShapeDtypeStruct((N_TOK, D_MODEL), F32),
        compiler_params=_cparams(("parallel", "arbitrary"), 52),
        name="matmul_residual",
    )(a, w, x, mods)


def _branch_merge_kernel(o0_ref, o1_ref, o2_ref, w0_ref, w1_ref, w2_ref, g0_ref, g1_ref, g2_ref, m_ref):
    m = jax.nn.sigmoid(g0_ref[...]) * jnp.dot(o0_ref[...], w0_ref[...], preferred_element_type=F32)
    m = m + jax.nn.sigmoid(g1_ref[...]) * jnp.dot(o1_ref[...], w1_ref[...], preferred_element_type=F32)
    m = m + jax.nn.sigmoid(g2_ref[...]) * jnp.dot(o2_ref[...], w2_ref[...], preferred_element_type=F32)
    m_ref[...] = m.astype(BF16)


def _branch_merge(o_ssm, o_win, o_na, w_branch, gates, layer):
    tm, tn = 1024, 512
    nj = D_MODEL // tn
    o_spec = pl.BlockSpec((tm, BRANCH_W), lambda i, j: (i, 0))
    w_spec = lambda k: pl.BlockSpec((None, None, BRANCH_W, tn), lambda i, j: (layer, k, 0, j))
    g_spec = lambda k: pl.BlockSpec((tm, tn), lambda i, j: (i, GATE_OFF // tn + k * nj + j))
    return pl.pallas_call(
        _branch_merge_kernel,
        grid=(N_TOK // tm, nj),
        in_specs=[o_spec, o_spec, o_spec, w_spec(0), w_spec(1), w_spec(2), g_spec(0), g_spec(1), g_spec(2)],
        out_specs=pl.BlockSpec((tm, tn), lambda i, j: (i, j)),
        out_shape=jax.ShapeDtypeStruct((N_TOK, D_MODEL), BF16),
        compiler_params=_cparams(("parallel", "arbitrary"), 48),
        name="branch_merge",
    )(o_ssm, o_win, o_na, w_branch, w_branch, w_branch, gates, gates, gates)


FFN_TM = 1024
FFN_TF = 512


def _ffn_up_kernel(x_ref, g_ref, sh_ref, sc_ref, wa_ref, wb_ref, cwa_ref, cwb_ref, cba_ref, cbb_ref,
                   o_ref, h_ref):
    i = pl.program_id(0)

    @pl.when(pl.program_id(1) == 0)
    def _():
        h_ref[...] = _norm_mod(x_ref[...], g_ref[...], sh_ref[...], sc_ref[...]).astype(BF16)

    seq = jnp.where(i < N_CTX // FFN_TM, SEQ, DEC_SEQ)
    pos = lax.broadcasted_iota(jnp.int32, (FFN_TM, 1), 0) & (seq - 1)
    first = pos == 0
    last = pos == seq - 1

    def conv(u, cw_ref, cb_ref):
        prev = jnp.where(first, 0.0, pltpu.roll(u, 1, 0))
        nxt = jnp.where(last, 0.0, pltpu.roll(u, FFN_TM - 1, 0))
        return prev * cw_ref[0:1, :] + u * cw_ref[1:2, :] + nxt * cw_ref[2:3, :] + cb_ref[...]

    h = h_ref[...]
    a = conv(jnp.dot(h, wa_ref[...], preferred_element_type=F32), cwa_ref, cba_ref)
    b = conv(jnp.dot(h, wb_ref[...], preferred_element_type=F32), cwb_ref, cbb_ref)
    o_ref[...] = (jax.nn.silu(a) * b).astype(BF16)


def _ffn_up(x, norm_g, mods, w_up, conv_w, conv_b, layer):
    tm, tf = FFN_TM, FFN_TF
    nj = D_FF // tf
    zero = lambda i, j: 0
    conv_b3 = conv_b.reshape(DEPTH, 1, 2 * D_FF)
    return pl.pallas_call(
        _ffn_up_kernel,
        grid=(N_TOK // tm, nj),
        in_specs=[pl.BlockSpec((tm, D_MODEL), lambda i, j: (i, 0)),
                  pl.BlockSpec((None, 1, D_MODEL), lambda i, j: (layer, 0, 0)),
                  _mod_spec(layer, 3, tm, D_MODEL, zero),
                  _mod_spec(layer, 4, tm, D_MODEL, zero),
                  pl.BlockSpec((None, D_MODEL, tf), lambda i, j: (layer, 0, j)),
                  pl.BlockSpec((None, D_MODEL, tf), lambda i, j: (layer, 0, nj + j)),
                  pl.BlockSpec((None, CONV_W, tf), lambda i, j: (layer, 0, j)),
                  pl.BlockSpec((None, CONV_W, tf), lambda i, j: (layer, 0, nj + j)),
                  pl.BlockSpec((None, 1, tf), lambda i, j: (layer, 0, j)),
                  pl.BlockSpec((None, 1, tf), lambda i, j: (layer, 0, nj + j))],
        out_specs=pl.BlockSpec((tm, tf), lambda i, j: (i, j)),
        out_shape=jax.ShapeDtypeStruct((N_TOK, D_FF), BF16),
        scratch_shapes=[pltpu.VMEM((tm, D_MODEL), BF16)],
        compiler_params=_cparams(("parallel", "arbitrary"), 56),
        name="ffn_up",
    )(x, norm_g.reshape(DEPTH, 1, D_MODEL), mods, mods, w_up, w_up, conv_w, conv_w, conv_b3, conv_b3)


def _rmsnorm_kernel(x_ref, g_ref, o_ref):
    x = x_ref[...]
    o_ref[...] = x * lax.rsqrt(jnp.mean(x * x, axis=-1, keepdims=True) + EPS) * g_ref[...]


def _final_norm(x, g, row0, rows):
    tm = 512
    return pl.pallas_call(
        _rmsnorm_kernel,
        grid=(rows // tm,),
        in_specs=[pl.BlockSpec((tm, D_MODEL), lambda i: (row0 // tm + i, 0)),
                  pl.BlockSpec((1, D_MODEL), lambda i: (0, 0))],
        out_specs=pl.BlockSpec((tm, D_MODEL), lambda i: (i, 0)),
        out_shape=jax.ShapeDtypeStruct((rows, D_MODEL), F32),
        compiler_params=_cparams(("parallel",), 32),
        name="final_norm",
    )(x, g.reshape(1, D_MODEL))


def _dot_nt(a, b):
    return lax.dot_general(a, b, (((1,), (1,)), ((), ())), preferred_element_type=F32)


def _lane_is_low(rows):
    return lax.broadcasted_iota(jnp.int32, (rows, LANES), 1) < HALF


def _one_head_query(q2, low, half, kv_half):
    qa = jnp.where(low if half == 0 else jnp.logical_not(low), q2, 0.0)
    if half != kv_half:
        qa = pltpu.roll(qa, HALF, 1)
    return qa.astype(BF16)


def _ctx_attn_kernel(sink_ref, qw_ref, kw_ref, vw_ref, qn_ref, kn_ref, vn_ref, ow_ref, on_ref, *, layer):
    low = _lane_is_low(SEQ)

    def attend(qa, k2, v2, sink):
        s = _dot_nt(qa, k2)
        m = jnp.max(s, axis=-1, keepdims=True)
        if sink is not None:
            m = jnp.maximum(m, sink)
        e = jnp.exp(s - m)
        den = jnp.sum(e, axis=-1, keepdims=True)
        if sink is not None:
            den = den + jnp.exp(sink - m)
        return jnp.dot(e.astype(BF16), v2, preferred_element_type=F32) / den

    for j in range(WIN_HEADS // 2):
        q2 = qw_ref[:, j * LANES:(j + 1) * LANES] * ATT_SCALE
        outs = []
        for half in range(2):
            head = 2 * j + half
            kv = head // WIN_GROUP
            kt, kv_half = kv // 2, kv % 2
            k2 = kw_ref[:, kt * LANES:(kt + 1) * LANES].astype(BF16)
            v2 = vw_ref[:, kt * LANES:(kt + 1) * LANES].astype(BF16)
            o = attend(_one_head_query(q2, low, half, kv_half), k2, v2, sink_ref[layer, head])
            if half != kv_half:
                o = pltpu.roll(o, HALF, 1)
            outs.append(o)
        ow_ref[:, j * LANES:(j + 1) * LANES] = jnp.where(low, outs[0], outs[1]).astype(BF16)

    for j in range(NA_HEADS // 2):
        q2 = qn_ref[:, j * LANES:(j + 1) * LANES] * ATT_SCALE
        k2 = kn_ref[:, j * LANES:(j + 1) * LANES].astype(BF16)
        v2 = vn_ref[:, j * LANES:(j + 1) * LANES].astype(BF16)
        outs = [attend(_one_head_query(q2, low, half, half), k2, v2, None) for half in range(2)]
        on_ref[:, j * LANES:(j + 1) * LANES] = jnp.where(low, outs[0], outs[1]).astype(BF16)


def _ctx_attention(proj, win_sink, layer):
    w768 = lambda off: pl.BlockSpec((SEQ, 768), lambda b: (b, off // 768))
    w256 = lambda off: pl.BlockSpec((SEQ, WIN_KV), lambda b: (b, off // WIN_KV))
    out_spec = pl.BlockSpec((SEQ, 768), lambda b: (b, 0))
    return pl.pallas_call(
        functools.partial(_ctx_attn_kernel, layer=layer),
        grid=(BATCH,),
        in_specs=[pl.BlockSpec(memory_space=pltpu.SMEM),
                  w768(QW_OFF), w256(KW_OFF), w256(VW_OFF), w768(QN_OFF), w768(KN_OFF), w768(VN_OFF)],
        out_specs=[out_spec, out_spec],
        out_shape=[jax.ShapeDtypeStruct((N_TOK, 768), BF16)] * 2,
        compiler_params=_cparams(("parallel",), 32),
        name="ctx_attention",
    )(win_sink, proj, proj, proj, proj, proj, proj)


def _rope_tables():
    nf = HEAD_DIM // 4
    pos = np.arange(DEC_SEQ)
    row = (pos // GRID_W).astype(np.float32)
    col = (pos % GRID_W).astype(np.float32)
    inv = (np.float32(ROPE_BASE) ** (-np.arange(nf, dtype=np.float32) / np.float32(nf))).astype(np.float32)
    ang_row = (row[:, None] * inv[None, :]).astype(np.float32)
    ang_col = (col[:, None] * inv[None, :]).astype(np.float32)
    cos_h = np.concatenate([np.cos(ang_row), np.cos(ang_row), np.cos(ang_col), np.cos(ang_col)], axis=-1)
    sin_h = np.concatenate([-np.sin(ang_row), np.sin(ang_row), -np.sin(ang_col), np.sin(ang_col)], axis=-1)
    cos_q = np.tile(cos_h, (1, WIN_HEADS)).astype(np.float32)
    sin_q = np.tile(sin_h, (1, WIN_HEADS)).astype(np.float32)
    return cos_q, sin_q


def _rope(x, cos, sin_signed):
    width = x.shape[-1]
    q = HEAD_DIM // 4
    lane = lax.broadcasted_iota(jnp.int32, x.shape, 1)
    partner = jnp.where((lane & (2 * q - 1)) < q, pltpu.roll(x, width - q, 1), pltpu.roll(x, q, 1))
    return x * cos + partner * sin_signed


def _win_attn_kernel(sink_ref, q_ref, kp_ref, kc_ref, kn_ref, vp_ref, vc_ref, vn_ref, ck_ref, cv_ref,
                     cosq_ref, sinq_ref, cosp_ref, sinp_ref, cosc_ref, sinc_ref, cosn_ref, sinn_ref,
                     o_prev_ref, o_ref, *, layer):
    del o_prev_ref
    n = pl.program_id(1)
    nb = DEC_SEQ // WIN_BLOCK
    low = _lane_is_low(WIN_BLOCK)

    q = _rope(q_ref[...], cosq_ref[...], sinq_ref[...]) * ATT_SCALE
    k_loc = jnp.concatenate([_rope(kp_ref[...], cosp_ref[...], sinp_ref[...]),
                             _rope(kc_ref[...], cosc_ref[...], sinc_ref[...]),
                             _rope(kn_ref[...], cosn_ref[...], sinn_ref[...])], axis=0).astype(BF16)
    v_loc = jnp.concatenate([vp_ref[...], vc_ref[...], vn_ref[...]], axis=0).astype(BF16)
    k_ctx = ck_ref[...].astype(BF16)
    v_ctx = cv_ref[...].astype(BF16)

    qi = lax.broadcasted_iota(jnp.int32, (WIN_BLOCK, 3 * WIN_BLOCK), 0)
    kj = lax.broadcasted_iota(jnp.int32, (WIN_BLOCK, 3 * WIN_BLOCK), 1)
    rel = kj - WIN_BLOCK - qi
    valid = jnp.logical_and(rel >= -WINDOW, rel <= WINDOW)
    valid = jnp.logical_and(valid, kj >= jnp.where(n > 0, 0, WIN_BLOCK))
    valid = jnp.logical_and(valid, kj < jnp.where(n < nb - 1, 3 * WIN_BLOCK, 2 * WIN_BLOCK))

    for j in range(WIN_HEADS // 2):
        q2 = q[:, j * LANES:(j + 1) * LANES]
        outs = []
        for half in range(2):
            head = 2 * j + half
            kv = head // WIN_GROUP
            kt, kv_half = kv // 2, kv % 2
            cols = slice(kt * LANES, (kt + 1) * LANES)
            qa = _one_head_query(q2, low, half, kv_half)
            s_loc = jnp.where(valid, _dot_nt(qa, k_loc[:, cols]), NEG_INF)
            s_ctx = _dot_nt(qa, k_ctx[:, cols])
            sink = sink_ref[layer, head]
            m = jnp.maximum(jnp.max(s_loc, axis=-1, keepdims=True), jnp.max(s_ctx, axis=-1, keepdims=True))
            m = jnp.maximum(m, sink)
            e_loc = jnp.exp(s_loc - m)
            e_ctx = jnp.exp(s_ctx - m)
            den = (jnp.sum(e_loc, axis=-1, keepdims=True) + jnp.sum(e_ctx, axis=-1, keepdims=True)
                   + jnp.exp(sink - m))
            o = (jnp.dot(e_loc.astype(BF16), v_loc[:, cols], preferred_element_type=F32)
                 + jnp.dot(e_ctx.astype(BF16), v_ctx[:, cols], preferred_element_type=F32)) / den
            if half != kv_half:
                o = pltpu.roll(o, HALF, 1)
            outs.append(o)
        o_ref[:, j * LANES:(j + 1) * LANES] = jnp.where(low, outs[0], outs[1]).astype(BF16)


def _win_attention(proj, cache_k, cache_v, win_sink, tables, o_ctx, layer):
    nb = DEC_SEQ // WIN_BLOCK
    row0 = N_CTX // WIN_BLOCK
    cos_q, sin_q, cos_k, sin_k = tables
    prev = lambda n: jnp.maximum(n - 1, 0)
    cur = lambda n: n
    nxt = lambda n: jnp.minimum(n + 1, nb - 1)
    kv_spec = lambda off, f: pl.BlockSpec((WIN_BLOCK, WIN_KV), lambda b, n: (row0 + b * nb + f(n), off // WIN_KV))
    tab_q = pl.BlockSpec((WIN_BLOCK, 768), lambda b, n: (n, 0))
    tab_k = lambda f: pl.BlockSpec((WIN_BLOCK, WIN_KV), lambda b, n: (f(n), 0))
    cache_spec = pl.BlockSpec((None, None, PAST_LEN, WIN_KV), lambda b, n: (b, layer, 0, 0))
    return pl.pallas_call(
        functools.partial(_win_attn_kernel, layer=layer),
        grid=(DEC_BATCH, nb),
        in_specs=[pl.BlockSpec(memory_space=pltpu.SMEM),
                  pl.BlockSpec((WIN_BLOCK, 768), lambda b, n: (row0 + b * nb + n, QW_OFF // 768)),
                  kv_spec(KW_OFF, prev), kv_spec(KW_OFF, cur), kv_spec(KW_OFF, nxt),
                  kv_spec(VW_OFF, prev), kv_spec(VW_OFF, cur), kv_spec(VW_OFF, nxt),
                  cache_spec, cache_spec,
                  tab_q, tab_q, tab_k(prev), tab_k(prev), tab_k(cur), tab_k(cur), tab_k(nxt), tab_k(nxt),
                  pl.BlockSpec(memory_space=pl.ANY)],
        out_specs=pl.BlockSpec((WIN_BLOCK, 768), lambda b, n: (row0 + b * nb + n, 0)),
        out_shape=jax.ShapeDtypeStruct((N_TOK, 768), BF16),
        input_output_aliases={18: 0},
        compiler_params=_cparams(("parallel", "parallel"), 32),
        name="win_attention",
    )(win_sink, proj, proj, proj, proj, proj, proj, proj, cache_k, cache_v,
      cos_q, sin_q, cos_k, sin_k, cos_k, sin_k, cos_k, sin_k, o_ctx)


def _na_window_start(r0):
    return min(max(r0 - NA_ROWS // 2, 0), GRID_H - NA_WIN_ROWS)


def _na_bias_kernel(rpb_ref, o_ref):
    head = pl.program_id(0)
    base = head * (RPB_H * RPB_W)
    qc = lax.broadcasted_iota(jnp.int32, (GRID_W, LANES), 0)
    lane = lax.broadcasted_iota(jnp.int32, (GRID_W, LANES), 1)
    kc = lane & (GRID_W - 1)
    low = lane < HALF
    dc_idx = jnp.clip(kc - qc + NA_COLS - 1, 0, RPB_W - 1)
    q_start = jnp.clip(qc - NA_COLS // 2, 0, GRID_W - NA_COLS)
    col_valid = jnp.logical_and(kc >= q_start, kc < q_start + NA_COLS)
    neg = jnp.full((GRID_W, LANES), NEG_INF, F32)

    pair = {}
    for d in range(-NA_ROWS, NA_ROWS):
        acc = jnp.zeros((GRID_W, LANES), F32)
        for dc in range(RPB_W):
            s_lo = rpb_ref[base + (d + NA_ROWS - 1) * RPB_W + dc] if abs(d) < NA_ROWS else 0.0
            s_hi = rpb_ref[base + (d + NA_ROWS) * RPB_W + dc] if abs(d + 1) < NA_ROWS else 0.0
            acc = jnp.where(dc_idx == dc, jnp.where(low, s_lo, s_hi), acc)
        pair[d] = jnp.where(col_valid, acc, NEG_INF)

    for qb in range(NA_QBLOCKS):
        r0 = NA_QROWS * qb
        ws = _na_window_start(r0)
        for qi in range(NA_QROWS):
            qr = r0 + qi
            rs = min(max(qr - NA_ROWS // 2, 0), GRID_H - NA_ROWS)
            for p in range(NA_WIN_ROWS // 2):
                kr = ws + 2 * p
                ok_lo = rs <= kr < rs + NA_ROWS
                ok_hi = rs <= kr + 1 < rs + NA_ROWS
                d = kr - qr
                if not (ok_lo or ok_hi):
                    tile = neg
                else:
                    tile = pair[d]
                    if not ok_lo:
                        tile = jnp.where(low, NEG_INF, tile)
                    if not ok_hi:
                        tile = jnp.where(low, tile, NEG_INF)
                o_ref[qb, qi * GRID_W:(qi + 1) * GRID_W, p * LANES:(p + 1) * LANES] = tile


def _na_bias(rpb_l):
    return pl.pallas_call(
        _na_bias_kernel,
        grid=(NA_HEADS,),
        in_specs=[pl.BlockSpec(memory_space=pltpu.SMEM)],
        out_specs=pl.BlockSpec((None, NA_QBLOCKS, NA_QROWS * GRID_W, NA_WIN_KEYS), lambda h: (h, 0, 0, 0)),
        out_shape=jax.ShapeDtypeStruct((NA_HEADS, NA_QBLOCKS, NA_QROWS * GRID_W, NA_WIN_KEYS), F32),
        compiler_params=_cparams(("parallel",), 32),
        name="na_bias",
    )(rpb_l.reshape(NA_HEADS * RPB_H * RPB_W))


def _na_attn_kernel(q_ref, k_ref, v_ref, ck_ref, cv_ref, bias_ref, o_prev_ref, o_ref):
    del o_prev_ref
    qb = pl.program_id(2)
    nq = NA_QROWS * GRID_W
    low = _lane_is_low(nq)
    ws = jnp.clip(NA_QROWS * qb - NA_ROWS // 2, 0, GRID_H - NA_WIN_ROWS)
    start = pl.multiple_of(ws * GRID_W, LANES)
    k_loc = k_ref[pl.ds(start, NA_WIN_KEYS), :].astype(BF16)
    v_loc = v_ref[pl.ds(start, NA_WIN_KEYS), :].astype(BF16)
    k_ctx = ck_ref[...].astype(BF16)
    v_ctx = cv_ref[...].astype(BF16)
    q2 = q_ref[...] * ATT_SCALE
    outs = []
    for half in range(2):
        qa = _one_head_query(q2, low, half, half)
        s_loc = _dot_nt(qa, k_loc) + bias_ref[half]
        s_ctx = _dot_nt(qa, k_ctx)
        m = jnp.maximum(jnp.max(s_loc, axis=-1, keepdims=True), jnp.max(s_ctx, axis=-1, keepdims=True))
        e_loc = jnp.exp(s_loc - m)
        e_ctx = jnp.exp(s_ctx - m)
        den = jnp.sum(e_loc, axis=-1, keepdims=True) + jnp.sum(e_ctx, axis=-1, keepdims=True)
        outs.append((jnp.dot(e_loc.astype(BF16), v_loc, preferred_element_type=F32)
                     + jnp.dot(e_ctx.astype(BF16), v_ctx, preferred_element_type=F32)) / den)
    o_ref[...] = jnp.where(low, outs[0], outs[1]).astype(BF16)


def _na_attention(proj, cache_k, cache_v, bias, o_ctx, layer):
    nq = NA_QROWS * GRID_W
    seq_blk0 = N_CTX // DEC_SEQ
    q_blk0 = N_CTX // nq
    kv_spec = lambda off: pl.BlockSpec((DEC_SEQ, LANES), lambda hp, b, qb: (seq_blk0 + b, off // LANES + hp))
    cache_spec = pl.BlockSpec((None, None, PAST_LEN, LANES), lambda hp, b, qb: (b, layer, 0, hp))
    return pl.pallas_call(
        _na_attn_kernel,
        grid=(NA_HEADS // 2, DEC_BATCH, NA_QBLOCKS),
        in_specs=[pl.BlockSpec((nq, LANES), lambda hp, b, qb: (q_blk0 + b * NA_QBLOCKS + qb, QN_OFF // LANES + hp)),
                  kv_spec(KN_OFF), kv_spec(VN_OFF), cache_spec, cache_spec,
                  pl.BlockSpec((2, None, nq, NA_WIN_KEYS), lambda hp, b, qb: (hp, qb, 0, 0)),
                  pl.BlockSpec(memory_space=pl.ANY)],
        out_specs=pl.BlockSpec((nq, LANES), lambda hp, b, qb: (q_blk0 + b * NA_QBLOCKS + qb, hp)),
        out_shape=jax.ShapeDtypeStruct((N_TOK, 768), BF16),
        input_output_aliases={6: 0},
        compiler_params=_cparams(("parallel", "parallel", "parallel"), 32),
        name="na_attention",
    )(proj, proj, proj, cache_k, cache_v, bias, o_ctx)


def _ssm_prep_kernel(lr_ref, li_ref, ldt_ref, br_ref, bi_ref, lbr_ref, lbi_ref, bbr_ref, bbi_ref):
    lr, li = lr_ref[...], li_ref[...]
    dt = jnp.exp(ldt_ref[...])
    mag = jnp.exp(lr * dt)
    ang = li * dt
    lbr = mag * jnp.cos(ang)
    lbi = mag * jnp.sin(ang)
    nr = lbr - 1.0
    den = lr * lr + li * li
    cr = (nr * lr + lbi * li) / den
    ci = (lbi * lr - nr * li) / den
    br, bi = br_ref[...], bi_ref[...]
    lbr_ref[...] = lbr
    lbi_ref[...] = lbi
    bbr_ref[...] = cr * br - ci * bi
    bbi_ref[...] = cr * bi + ci * br


def _ssm_prep(lam_re, lam_im, log_dt, b_re, b_im):
    shape5 = (DEPTH, 2, SSM_GROUPS, SSM_GROUP, SSM_STATE)
    rows = DEPTH * 2 * SSM_GROUPS * SSM_GROUP
    bc = lambda t: jnp.broadcast_to(t, shape5).reshape(rows, SSM_STATE)
    args = (bc(lam_re[:, :, :, None, :]), bc(lam_im[:, :, :, None, :]), bc(log_dt[:, :, :, None, None]),
            jnp.swapaxes(b_re, -1, -2).reshape(rows, SSM_STATE), jnp.swapaxes(b_im, -1, -2).reshape(rows, SSM_STATE))
    outs = pl.pallas_call(
        _ssm_prep_kernel,
        out_shape=[jax.ShapeDtypeStruct((rows, SSM_STATE), F32)] * 4,
        compiler_params=_cparams(None, 32),
        name="ssm_prep",
    )(*args)
    lbr, lbi, bbr, bbi = [o.reshape(shape5) for o in outs]
    return lbr[:, :, :, 0], lbi[:, :, :, 0], bbr, bbi


def _scan_permutation():
    p = np.zeros((SCAN_ROWS, SCAN_ROWS), np.float32)
    for t in range(SCAN_T):
        for s in range(SUBLANES):
            src_t = t if s < SUBLANES // 2 else SCAN_T - 1 - t
            p[t * SUBLANES + s, s * SCAN_T + src_t] = 1.0
    return p


def _ssm_scan_kernel(*refs, aliased):
    uf_ref, ub_ref, h0_ref, lam_ref, perm_ref, permt_ref, b_ref, c_ref = refs[:8]
    yf_ref, yb_ref, fin_ref, st_ref, bu_ref, y_ref = refs[8 + (2 if aliased else 0):]
    half_rows = SCAN_ROWS // 2

    @pl.when(pl.program_id(1) == 0)
    def _():
        st_ref[...] = h0_ref[...]

    u_bm = jnp.concatenate([uf_ref[...].reshape(half_rows, SSM_WIDTH),
                            ub_ref[...].reshape(half_rows, SSM_WIDTH)], axis=0).astype(BF16)
    u_tm = jnp.dot(perm_ref[...], u_bm, preferred_element_type=F32).astype(BF16)

    fwd = (lax.broadcasted_iota(jnp.int32, (SCAN_ROWS, 1), 0) & (SUBLANES - 1)) < SUBLANES // 2
    zero = jnp.zeros((SCAN_ROWS, LANES), BF16)
    for kb in range(SSM_KB):
        ub = u_tm[:, kb * LANES:(kb + 1) * LANES]
        lhs = jnp.concatenate([jnp.where(fwd, ub, zero), jnp.where(fwd, zero, ub)], axis=1)
        bu_ref[...] = jnp.dot(lhs, b_ref[kb], preferred_element_type=F32)
        ar = lam_ref[kb, 0]
        ai = lam_ref[kb, 1]

        def step(t, carry):
            xr, xi = carry
            r = pl.multiple_of(t * SUBLANES, SUBLANES)
            nr = ar * xr - ai * xi + bu_ref[pl.ds(r, SUBLANES), 0:STATE_COLS]
            ni = ar * xi + ai * xr + bu_ref[pl.ds(r, SUBLANES), STATE_COLS:2 * STATE_COLS]
            bu_ref[pl.ds(r, SUBLANES), 0:STATE_COLS] = nr
            bu_ref[pl.ds(r, SUBLANES), STATE_COLS:2 * STATE_COLS] = ni
            return nr, ni

        xr, xi = lax.fori_loop(0, SCAN_T, step, (st_ref[kb, 0], st_ref[kb, 1]))
        st_ref[kb, 0] = xr
        st_ref[kb, 1] = xi
        xs = bu_ref[...].astype(BF16)
        yy = jnp.dot(xs, c_ref[kb], preferred_element_type=F32)
        y_ref[:, kb * LANES:(kb + 1) * LANES] = jnp.where(fwd, yy[:, :LANES], yy[:, LANES:])
    fin_ref[...] = st_ref[...]

    y = y_ref[...]
    hi = y.astype(BF16)
    rest = y - hi.astype(F32)
    mid = rest.astype(BF16)
    low = (rest - mid.astype(F32)).astype(BF16)
    permt = permt_ref[...]
    y_bm = (jnp.dot(permt, hi, preferred_element_type=F32) + jnp.dot(permt, mid, preferred_element_type=F32)
            + jnp.dot(permt, low, preferred_element_type=F32))
    yf_ref[...] = y_bm[:half_rows].reshape(SUBLANES // 2, SCAN_T, SSM_WIDTH)
    yb_ref[...] = y_bm[half_rows:].reshape(SUBLANES // 2, SCAN_T, SSM_WIDTH)


def _ssm_scan(proj, seq_len, first_seq, n_seq, h0, lam, bmat, cmat, y_prev=None):
    n_all = N_TOK // seq_len
    nc = seq_len // SCAN_T
    blk0 = first_seq // 4
    proj3 = proj.reshape(n_all, seq_len, N_PROJ)
    state_shape = (SSM_KB, 2, SUBLANES, STATE_COLS)
    whole = lambda shape: pl.BlockSpec(shape, lambda b, c: (0,) * len(shape))
    state_spec = pl.BlockSpec((None,) + state_shape, lambda b, c: (b, 0, 0, 0, 0))
    u_spec = lambda chunk: pl.BlockSpec((4, SCAN_T, SSM_WIDTH), lambda b, c: (blk0 + b, chunk(c), U_OFF // SSM_WIDTH))
    y_spec = lambda chunk: pl.BlockSpec((4, SCAN_T, SSM_WIDTH), lambda b, c: (blk0 + b, chunk(c), 0))
    ascending = lambda c: c
    descending = lambda c: nc - 1 - c
    perm = _scan_permutation()
    aliased = y_prev is not None
    extra_specs = [pl.BlockSpec(memory_space=pl.ANY)] * 2 if aliased else []
    extra_args = [t.reshape(n_all, seq_len, SSM_WIDTH) for t in y_prev] if aliased else []
    yf, yb, fin = pl.pallas_call(
        functools.partial(_ssm_scan_kernel, aliased=aliased),
        grid=(n_seq // 4, nc),
        in_specs=[u_spec(ascending), u_spec(descending), state_spec, whole(state_shape),
                  whole((SCAN_ROWS, SCAN_ROWS)), whole((SCAN_ROWS, SCAN_ROWS)),
                  whole((SSM_KB, 2 * LANES, 2 * STATE_COLS)), whole((SSM_KB, 2 * STATE_COLS, 2 * LANES))] + extra_specs,
        out_specs=[y_spec(ascending), y_spec(descending), state_spec],
        out_shape=[jax.ShapeDtypeStruct((n_all, seq_len, SSM_WIDTH), F32)] * 2
                  + [jax.ShapeDtypeStruct((n_seq // 4,) + state_shape, F32)],
        scratch_shapes=[pltpu.VMEM(state_shape, F32), pltpu.VMEM((SCAN_ROWS, 2 * STATE_COLS), F32),
                        pltpu.VMEM((SCAN_ROWS, SSM_WIDTH), F32)],
        input_output_aliases={8: 0, 9: 1} if aliased else {},
        compiler_params=_cparams(("parallel", "arbitrary"), 40),
        name="ssm_scan",
    )(proj3, proj3, h0, lam, jnp.asarray(perm, BF16), jnp.asarray(perm.T, BF16), bmat, cmat, *extra_args)
    return yf.reshape(N_TOK, SSM_WIDTH), yb.reshape(N_TOK, SSM_WIDTH), fin


def _ssm_scan_params(lbr, lbi, bbr, bbi, c_re, c_im):
    eye = jnp.eye(GROUPS_PER_KB, dtype=F32)

    def lam_tile(t):
        t = t.reshape(DEPTH, 2, SSM_KB, STATE_COLS).transpose(0, 2, 1, 3)
        return jnp.repeat(t, SUBLANES // 2, axis=2)

    def b_blocks(t):
        t = t.reshape(DEPTH, 2, SSM_KB, GROUPS_PER_KB, SSM_GROUP, SSM_STATE)
        return jnp.einsum('ldkgnp,gh->ldkgnhp', t, eye).reshape(DEPTH, 2, SSM_KB, LANES, STATE_COLS)

    def c_blocks(t):
        t = t.reshape(DEPTH, 2, SSM_KB, GROUPS_PER_KB, SSM_GROUP, SSM_STATE)
        return jnp.einsum('ldkgnp,gh->ldkgphn', t, eye).reshape(DEPTH, 2, SSM_KB, STATE_COLS, LANES)

    lam = jnp.stack([lam_tile(lbr), lam_tile(lbi)], axis=2)
    b = jnp.concatenate([b_blocks(bbr), b_blocks(bbi)], axis=-1)
    bmat = jnp.concatenate([b[:, 0], b[:, 1]], axis=-2).astype(BF16)
    c = jnp.concatenate([c_blocks(c_re), -c_blocks(c_im)], axis=-2)
    cmat = jnp.concatenate([c[:, 0], c[:, 1]], axis=-1).astype(BF16)
    return lam, bmat, cmat


def _ssm_glu_kernel(yf_ref, yb_ref, u_ref, d_ref, w_ref, b_ref, o_ref):
    y = yf_ref[...] + yb_ref[...] + d_ref[...] * u_ref[...]
    y = jax.nn.gelu(y)
    z = jnp.dot(y.astype(BF16), w_ref[...], preferred_element_type=F32) + b_ref[...]
    o_ref[...] = (y * jax.nn.sigmoid(z)).astype(BF16)


def _ssm_glu(yf, yb, proj, d_skip, w_glu, b_glu, layer):
    tm = 512
    row = pl.BlockSpec((tm, SSM_WIDTH), lambda i: (i, 0))
    vec = pl.BlockSpec((None, 1, SSM_WIDTH), lambda i: (layer, 0, 0))
    return pl.pallas_call(
        _ssm_glu_kernel,
        grid=(N_TOK // tm,),
        in_specs=[row, row, pl.BlockSpec((tm, SSM_WIDTH), lambda i: (i, U_OFF // SSM_WIDTH)), vec,
                  pl.BlockSpec((None, SSM_WIDTH, SSM_WIDTH), lambda i: (layer, 0, 0)), vec],
        out_specs=row,
        out_shape=jax.ShapeDtypeStruct((N_TOK, SSM_WIDTH), BF16),
        compiler_params=_cparams(("parallel",), 32),
        name="ssm_glu",
    )(yf, yb, proj, d_skip.reshape(DEPTH, 1, SSM_WIDTH), w_glu, b_glu.reshape(DEPTH, 1, SSM_WIDTH))


def _ssm_mixer(proj, lam, bmat, cmat, h0_lat, d_skip, w_glu, b_glu, layer):
    h0_ctx = jnp.zeros((BATCH // 4, SSM_KB, 2, SUBLANES, STATE_COLS), F32)
    yf, yb, fin = _ssm_scan(proj, SEQ, 0, BATCH, h0_ctx, lam, bmat, cmat)
    yf, yb, _ = _ssm_scan(proj, DEC_SEQ, N_CTX // DEC_SEQ, DEC_BATCH, h0_lat, lam, bmat, cmat, y_prev=(yf, yb))
    return _ssm_glu(yf, yb, proj, d_skip, w_glu, b_glu, layer), fin


def _state_to_tiles(s_re, s_im):
    def one(t):
        t = t.transpose(1, 0, 2, 3).reshape(SUBLANES, SSM_KB, STATE_COLS)
        return t.transpose(1, 0, 2)
    return jnp.stack([one(s_re), one(s_im)], axis=1)[None]


def _tiles_to_state(fin, part):
    t = fin[:, :, part].reshape(BATCH // 4, SSM_KB, 2, 4, GROUPS_PER_KB, SSM_STATE)
    return t.transpose(0, 3, 2, 1, 4, 5).reshape(BATCH, 2, SSM_GROUPS, SSM_STATE)


def kernel(x_prompt, x_sample, cache_win_k, cache_win_v, cache_na_k, cache_na_v, state_ssm_re, state_ssm_im,
           c, c_ctx, norm1_g, norm2_g, w_mod, b_mod, w_in, ssm_lam_re, ssm_lam_im, ssm_log_dt,
           ssm_b_re, ssm_b_im, ssm_c_re, ssm_c_im, ssm_d, w_glu, b_glu, win_sink, na_rpb,
           w_branch, w_out, w_up, conv_w, conv_b, w_down, final_g):
    x = jnp.concatenate([x_prompt.reshape(N_CTX, D_MODEL), x_sample.reshape(N_LAT, D_MODEL)], axis=0)

    w_in_b = jnp.concatenate([w_in[:, :, 4352:], w_in[:, :, 0:1536], w_in[:, :, 2048:4352], w_in[:, :, 1536:2048],
                              jnp.zeros((DEPTH, D_MODEL, N_PROJ - N_IN), F32)], axis=-1).astype(BF16)
    w_mod_b, w_glu_b, w_branch_b = w_mod.astype(BF16), w_glu.astype(BF16), w_branch.astype(BF16)
    w_out_b, w_up_b, w_down_b = w_out.astype(BF16), w_up.astype(BF16), w_down.astype(BF16)

    cond8 = jnp.concatenate([c_ctx[None], c, jnp.zeros((SUBLANES - 1 - DEC_BATCH, D_MODEL), F32)], axis=0)
    mods = _adaln(cond8, w_mod_b, b_mod)
    mods = mods.reshape(DEPTH, SUBLANES, 6, D_MODEL).transpose(0, 2, 1, 3)[:, :, :, None, :]

    lbr, lbi, bbr, bbi = _ssm_prep(ssm_lam_re, ssm_lam_im, ssm_log_dt, ssm_b_re, ssm_b_im)
    lam, bmat, cmat = _ssm_scan_params(lbr, lbi, bbr, bbi, ssm_c_re, ssm_c_im)

    cos_q, sin_q = _rope_tables()
    tables = (jnp.asarray(cos_q), jnp.asarray(sin_q), jnp.asarray(cos_q[:, :WIN_KV]), jnp.asarray(sin_q[:, :WIN_KV]))
    cwk = cache_win_k.reshape(DEC_BATCH, DEPTH, PAST_LEN, WIN_KV)
    cwv = cache_win_v.reshape(DEC_BATCH, DEPTH, PAST_LEN, WIN_KV)
    cnk = cache_na_k.reshape(DEC_BATCH, DEPTH, PAST_LEN, NA_HEADS * HEAD_DIM)
    cnv = cache_na_v.reshape(DEC_BATCH, DEPTH, PAST_LEN, NA_HEADS * HEAD_DIM)

    new_wk, new_wv, new_nk, new_nv, new_sre, new_sim = [], [], [], [], [], []
    for l in range(DEPTH):
        proj = _norm_mod_matmul(x, norm1_g, mods, w_in_b, l, tm=1024, tn=IN_TN)

        h0_lat = _state_to_tiles(state_ssm_re[:, l], state_ssm_im[:, l])
        o_ssm, fin = _ssm_mixer(proj, lam[l], bmat[l], cmat[l], h0_lat, ssm_d, w_glu_b, b_glu, l)

        o_win, o_na = _ctx_attention(proj, win_sink, l)
        o_win = _win_attention(proj, cwk, cwv, win_sink, tables, o_win, l)
        o_na = _na_attention(proj, cnk, cnv, _na_bias(na_rpb[l]), o_na, l)

        merged = _branch_merge(o_ssm, o_win, o_na, w_branch_b, proj, l)
        x = _matmul_residual(merged, w_out_b, x, mods, l, 2, tm=1024, tn=1024)
        act = _ffn_up(x, norm2_g, mods, w_up_b, conv_w, conv_b, l)
        x = _matmul_residual(act, w_down_b, x, mods, l, 5, tm=1024, tn=512)

        ctx = proj[:N_CTX]
        new_wk.append(ctx[:, KW_OFF:KW_OFF + WIN_KV].reshape(BATCH, SEQ, WIN_KV_HEADS, HEAD_DIM))
        new_wv.append(ctx[:, VW_OFF:VW_OFF + WIN_KV].reshape(BATCH, SEQ, WIN_KV_HEADS, HEAD_DIM))
        new_nk.append(ctx[:, KN_OFF:KN_OFF + 768].reshape(BATCH, SEQ, NA_HEADS, HEAD_DIM))
        new_nv.append(ctx[:, VN_OFF:VN_OFF + 768].reshape(BATCH, SEQ, NA_HEADS, HEAD_DIM))
        new_sre.append(_tiles_to_state(fin, 0))
        new_sim.append(_tiles_to_state(fin, 1))

    y_ctx = _final_norm(x, final_g, 0, N_CTX)
    y_lat = _final_norm(x, final_g, N_CTX, N_LAT)
    return (y_ctx.reshape(BATCH, SEQ, D_MODEL), y_lat.reshape(DEC_BATCH, DEC_SEQ, D_MODEL),
            jnp.stack(new_wk, axis=1), jnp.stack(new_wv, axis=1), jnp.stack(new_nk, axis=1),
            jnp.stack(new_nv, axis=1), jnp.stack(new_sre, axis=1), jnp.stack(new_sim, axis=1))
```

```python
import functools

import numpy as np
import jax
import jax.numpy as jnp
from jax import lax
from jax.experimental import pallas as pl
from jax.experimental.pallas import tpu as pltpu

F32 = jnp.float32
BF16 = jnp.bfloat16

D_MODEL = 2048
BATCH = 16
SEQ = 256
DEPTH = 4
DEC_BATCH = 4
DEC_SEQ = 1024
PAST_LEN = 512
GRID_W = 64
GRID_H = DEC_SEQ // GRID_W
HEAD_DIM = 64
SSM_WIDTH = 768
SSM_GROUP = 16
SSM_GROUPS = SSM_WIDTH // SSM_GROUP
SSM_STATE = 64
WIN_HEADS = 12
WIN_KV_HEADS = 4
WIN_GROUP = WIN_HEADS // WIN_KV_HEADS
WINDOW = 128
WIN_BLOCK = 128
NA_HEADS = 12
NA_ROWS = 8
NA_COLS = 16
BRANCH_W = 768
D_FF = 5632
CONV_W = 3
ROPE_BASE = 10000.0
EPS = 1e-6
NEG_INF = -1e30
ATT_SCALE = HEAD_DIM ** -0.5
WIN_KV = WIN_KV_HEADS * HEAD_DIM

N_CTX = BATCH * SEQ
N_LAT = DEC_BATCH * DEC_SEQ
N_TOK = N_CTX + N_LAT

LANES = 128
SUBLANES = 8
HALF = LANES // 2
MIB = 1024 * 1024

GATE_OFF = 0
U_OFF = 3 * D_MODEL
QW_OFF = U_OFF + 768
QN_OFF = QW_OFF + 768
KN_OFF = QN_OFF + 768
VN_OFF = KN_OFF + 768
KW_OFF = VN_OFF + 768
VW_OFF = KW_OFF + WIN_KV
N_IN = VW_OFF + WIN_KV
IN_TN = 1536
N_PROJ = -(-N_IN // IN_TN) * IN_TN

SSM_KB = SSM_WIDTH // LANES
GROUPS_PER_KB = LANES // SSM_GROUP
STATE_COLS = GROUPS_PER_KB * SSM_STATE
SCAN_T = 64
SCAN_ROWS = SCAN_T * SUBLANES

NA_QROWS = 2
NA_QBLOCKS = GRID_H // NA_QROWS
NA_WIN_ROWS = 10
NA_WIN_KEYS = NA_WIN_ROWS * GRID_W
RPB_H = 2 * NA_ROWS - 1
RPB_W = 2 * NA_COLS - 1


def _cparams(dims, vmem_mib):
    return pltpu.CompilerParams(dimension_semantics=dims, vmem_limit_bytes=int(vmem_mib * MIB))


def _mod_row(i, tm):
    tiles_ctx = N_CTX // tm
    per_batch = DEC_SEQ // tm
    return jnp.where(i < tiles_ctx, 0, 1 + (i - tiles_ctx) // per_batch)


def _mod_spec(layer, which, tm, tn, col_of):
    return pl.BlockSpec((None, None, None, 1, tn),
                        lambda i, j: (layer, which, _mod_row(i, tm), 0, col_of(i, j)))


def _adaln_kernel(c_ref, w_ref, b_ref, o_ref):
    s = jax.nn.silu(c_ref[...]).astype(BF16)
    o_ref[...] = jnp.dot(s, w_ref[...].astype(BF16), preferred_element_type=F32) + b_ref[...]


def _adaln(cond8, w_mod, b_mod):
    tn = 1024
    n = 6 * D_MODEL
    return pl.pallas_call(
        _adaln_kernel,
        grid=(DEPTH, n // tn),
        in_specs=[pl.BlockSpec((SUBLANES, D_MODEL), lambda l, j: (0, 0)),
                  pl.BlockSpec((None, D_MODEL, tn), lambda l, j: (l, 0, j)),
                  pl.BlockSpec((None, 1, tn), lambda l, j: (l, 0, j))],
        out_specs=pl.BlockSpec((None, SUBLANES, tn), lambda l, j: (l, 0, j)),
        out_shape=jax.ShapeDtypeStruct((DEPTH, SUBLANES, n), F32),
        compiler_params=_cparams(("parallel", "parallel"), 40),
        name="adaln",
    )(cond8, w_mod, b_mod.reshape(DEPTH, 1, n))


def _norm_mod(x, g, sh, sc):
    y = x * lax.rsqrt(jnp.mean(x * x, axis=-1, keepdims=True) + EPS)
    y = y * g
    return y * (1.0 + sc) + sh


def _norm_mod_matmul_kernel(x_ref, g_ref, sh_ref, sc_ref, w_ref, o_ref, h_ref):
    @pl.when(pl.program_id(1) == 0)
    def _():
        h_ref[...] = _norm_mod(x_ref[...], g_ref[...], sh_ref[...], sc_ref[...]).astype(BF16)

    o_ref[...] = jnp.dot(h_ref[...], w_ref[...], preferred_element_type=F32)


def _norm_mod_matmul(x, norm_g, mods, w, layer, tm, tn):
    n = w.shape[-1]
    zero = lambda i, j: 0
    return pl.pallas_call(
        _norm_mod_matmul_kernel,
        grid=(N_TOK // tm, n // tn),
        in_specs=[pl.BlockSpec((tm, D_MODEL), lambda i, j: (i, 0)),
                  pl.BlockSpec((None, 1, D_MODEL), lambda i, j: (layer, 0, 0)),
                  _mod_spec(layer, 0, tm, D_MODEL, zero),
                  _mod_spec(layer, 1, tm, D_MODEL, zero),
                  pl.BlockSpec((None, D_MODEL, tn), lambda i, j: (layer, 0, j))],
        out_specs=pl.BlockSpec((tm, tn), lambda i, j: (i, j)),
        out_shape=jax.ShapeDtypeStruct((N_TOK, n), F32),
        scratch_shapes=[pltpu.VMEM((tm, D_MODEL), BF16)],
        compiler_params=_cparams(("parallel", "arbitrary"), 56),
        name="norm_mod_matmul",
    )(x, norm_g.reshape(DEPTH, 1, D_MODEL), mods, mods, w)


def _matmul_residual_kernel(a_ref, w_ref, x_ref, g_ref, o_ref):
    acc = jnp.dot(a_ref[...], w_ref[...], preferred_element_type=F32)
    o_ref[...] = x_ref[...] + g_ref[...] * acc


def _matmul_residual(a, w, x, mods, layer, which, tm, tn):
    k = a.shape[-1]
    return pl.pallas_call(
        _matmul_residual_kernel,
        grid=(N_TOK // tm, D_MODEL // tn),
        in_specs=[pl.BlockSpec((tm, k), lambda i, j: (i, 0)),
                  pl.BlockSpec((None, k, tn), lambda i, j: (layer, 0, j)),
                  pl.BlockSpec((tm, tn), lambda i, j: (i, j)),
                  _mod_spec(layer, which, tm, tn, lambda i, j: j)],
        out_specs=pl.BlockSpec((tm, tn), lambda i, j: (i, j)),
        out_shape=jax.ShapeDtypeStruct((N_TOK, D_MODEL), F32),
        compiler_params=_cparams(("parallel", "arbitrary"), 52),
        name="matmul_residual",
    )(a, w, x, mods)


def _branch_merge_kernel(o0_ref, o1_ref, o2_ref, w0_ref, w1_ref, w2_ref, g0_ref, g1_ref, g2_ref, m_ref):
    m = jax.nn.sigmoid(g0_ref[...]) * jnp.dot(o0_ref[...], w0_ref[...], preferred_element_type=F32)
    m = m + jax.nn.sigmoid(g1_ref[...]) * jnp.dot(o1_ref[...], w1_ref[...], preferred_element_type=F32)
    m = m + jax.nn.sigmoid(g2_ref[...]) * jnp.dot(o2_ref[...], w2_ref[...], preferred_element_type=F32)
    m_ref[...] = m.astype(BF16)


def _branch_merge(o_ssm, o_win, o_na, w_branch, gates, layer):
    tm, tn = 1024, 512
    nj = D_MODEL // tn
    o_spec = pl.BlockSpec((tm, BRANCH_W), lambda i, j: (i, 0))
    w_spec = lambda k: pl.BlockSpec((None, None, BRANCH_W, tn), lambda i, j: (layer, k, 0, j))
    g_spec = lambda k: pl.BlockSpec((tm, tn), lambda i, j: (i, GATE_OFF // tn + k * nj + j))
    return pl.pallas_call(
        _branch_merge_kernel,
        grid=(N_TOK // tm, nj),
        in_specs=[o_spec, o_spec, o_spec, w_spec(0), w_spec(1), w_spec(2), g_spec(0), g_spec(1), g_spec(2)],
        out_specs=pl.BlockSpec((tm, tn), lambda i, j: (i, j)),
        out_shape=jax.ShapeDtypeStruct((N_TOK, D_MODEL), BF16),
        compiler_params=_cparams(("parallel", "arbitrary"), 48),
        name="branch_merge",
    )(o_ssm, o_win, o_na, w_branch, w_branch, w_branch, gates, gates, gates)


FFN_TM = 1024
FFN_TF = 512


def _ffn_up_kernel(x_ref, g_ref, sh_ref, sc_ref, wa_ref, wb_ref, cwa_ref, cwb_ref, cba_ref, cbb_ref,
                   o_ref, h_ref):
    i = pl.program_id(0)

    @pl.when(pl.program_id(1) == 0)
    def _():
        h_ref[...] = _norm_mod(x_ref[...], g_ref[...], sh_ref[...], sc_ref[...]).astype(BF16)

    seq = jnp.where(i < N_CTX // FFN_TM, SEQ, DEC_SEQ)
    pos = lax.broadcasted_iota(jnp.int32, (FFN_TM, 1), 0) & (seq - 1)
    first = pos == 0
    last = pos == seq - 1

    def conv(u, cw_ref, cb_ref):
        prev = jnp.where(first, 0.0, pltpu.roll(u, 1, 0))
        nxt = jnp.where(last, 0.0, pltpu.roll(u, FFN_TM - 1, 0))
        return prev * cw_ref[0:1, :] + u * cw_ref[1:2, :] + nxt * cw_ref[2:3, :] + cb_ref[...]

    h = h_ref[...]
    a = conv(jnp.dot(h, wa_ref[...], preferred_element_type=F32), cwa_ref, cba_ref)
    b = conv(jnp.dot(h, wb_ref[...], preferred_element_type=F32), cwb_ref, cbb_ref)
    o_ref[...] = (jax.nn.silu(a) * b).astype(BF16)


def _ffn_up(x, norm_g, mods, w_up, conv_w, conv_b, layer):
    tm, tf = FFN_TM, FFN_TF
    nj = D_FF // tf
    zero = lambda i, j: 0
    conv_b3 = conv_b.reshape(DEPTH, 1, 2 * D_FF)
    return pl.pallas_call(
        _ffn_up_kernel,
        grid=(N_TOK // tm, nj),
        in_specs=[pl.BlockSpec((tm, D_MODEL), lambda i, j: (i, 0)),
                  pl.BlockSpec((None, 1, D_MODEL), lambda i, j: (layer, 0, 0)),
                  _mod_spec(layer, 3, tm, D_MODEL, zero),
                  _mod_spec(layer, 4, tm, D_MODEL, zero),
                  pl.BlockSpec((None, D_MODEL, tf), lambda i, j: (layer, 0, j)),
                  pl.BlockSpec((None, D_MODEL, tf), lambda i, j: (layer, 0, nj + j)),
                  pl.BlockSpec((None, CONV_W, tf), lambda i, j: (layer, 0, j)),
                  pl.BlockSpec((None, CONV_W, tf), lambda i, j: (layer, 0, nj + j)),
                  pl.BlockSpec((None, 1, tf), lambda i, j: (layer, 0, j)),
                  pl.BlockSpec((None, 1, tf), lambda i, j: (layer, 0, nj + j))],
        out_specs=pl.BlockSpec((tm, tf), lambda i, j: (i, j)),
        out_shape=jax.ShapeDtypeStruct((N_TOK, D_FF), BF16),
        scratch_shapes=[pltpu.VMEM((tm, D_MODEL), BF16)],
        compiler_params=_cparams(("parallel", "arbitrary"), 56),
        name="ffn_up",
    )(x, norm_g.reshape(DEPTH, 1, D_MODEL), mods, mods, w_up, w_up, conv_w, conv_w, conv_b3, conv_b3)


def _rmsnorm_kernel(x_ref, g_ref, o_ref):
    x = x_ref[...]
    o_ref[...] = x * lax.rsqrt(jnp.mean(x * x, axis=-1, keepdims=True) + EPS) * g_ref[...]


def _final_norm(x, g, row0, rows):
    tm = 512
    return pl.pallas_call(
        _rmsnorm_kernel,
        grid=(rows // tm,),
        in_specs=[pl.BlockSpec((tm, D_MODEL), lambda i: (row0 // tm + i, 0)),
                  pl.BlockSpec((1, D_MODEL), lambda i: (0, 0))],
        out_specs=pl.BlockSpec((tm, D_MODEL), lambda i: (i, 0)),
        out_shape=jax.ShapeDtypeStruct((rows, D_MODEL), F32),
        compiler_params=_cparams(("parallel",), 32),
        name="final_norm",
    )(x, g.reshape(1, D_MODEL))


def _dot_nt(a, b):
    return lax.dot_general(a, b, (((1,), (1,)), ((), ())), preferred_element_type=F32)


def _lane_is_low(rows):
    return lax.broadcasted_iota(jnp.int32, (rows, LANES), 1) < HALF


def _one_head_query(q2, low, half, kv_half):
    qa = jnp.where(low if half == 0 else jnp.logical_not(low), q2, 0.0)
    if half != kv_half:
        qa = pltpu.roll(qa, HALF, 1)
    return qa.astype(BF16)


def _attend(qa, segments, sink=None):
    logits = []
    for k, _, fix in segments:
        s = _dot_nt(qa, k)
        logits.append(s if fix is None else fix(s))
    m = functools.reduce(jnp.maximum, [jnp.max(s, axis=-1, keepdims=True) for s in logits])
    if sink is not None:
        m = jnp.maximum(m, sink)
    es = [jnp.exp(s - m) for s in logits]
    den = functools.reduce(jnp.add, [jnp.sum(e, axis=-1, keepdims=True) for e in es])
    if sink is not None:
        den = den + jnp.exp(sink - m)
    o = functools.reduce(jnp.add, [jnp.dot(e.astype(BF16), v, preferred_element_type=F32)
                                   for e, (_, v, _) in zip(es, segments)])
    return o / den


def _gqa_heads(q_tile, segments_of, sink_ref, layer, low, rows):
    head_out = {}
    for kv in range(WIN_KV_HEADS):
        kt, kv_half = kv // 2, kv % 2
        heads = [kv * WIN_GROUP + g for g in range(WIN_GROUP)]
        qa = jnp.concatenate([_one_head_query(q_tile(h // 2), low, h % 2, kv_half) for h in heads], axis=0)
        sink = jnp.concatenate([jnp.full((rows, 1), sink_ref[layer, h], F32) for h in heads], axis=0)
        o = _attend(qa, segments_of(kt), sink)
        for g, h in enumerate(heads):
            og = o[g * rows:(g + 1) * rows]
            head_out[h] = og if h % 2 == kv_half else pltpu.roll(og, HALF, 1)
    return jnp.concatenate([jnp.where(low, head_out[2 * j], head_out[2 * j + 1]).astype(BF16)
                            for j in range(WIN_HEADS // 2)], axis=1)


def _pair_heads(q2, segments, low, rows):
    qa = jnp.concatenate([_one_head_query(q2, low, 0, 0), _one_head_query(q2, low, 1, 1)], axis=0)
    o = _attend(qa, segments)
    return jnp.where(low, o[:rows], o[rows:]).astype(BF16)


def _ctx_attn_kernel(sink_ref, qw_ref, kw_ref, vw_ref, qn_ref, kn_ref, vn_ref, ow_ref, on_ref, *, layer):
    low = _lane_is_low(SEQ)
    tile = lambda ref, j: ref[:, j * LANES:(j + 1) * LANES]

    for j in range(WIN_HEADS // 2):
        q2 = tile(qw_ref, j) * ATT_SCALE
        outs = []
        for half in range(2):
            head = 2 * j + half
            kv = head // WIN_GROUP
            kt, kv_half = kv // 2, kv % 2
            segment = [(tile(kw_ref, kt).astype(BF16), tile(vw_ref, kt).astype(BF16), None)]
            sink = jnp.full((SEQ, 1), sink_ref[layer, head], F32)
            o = _attend(_one_head_query(q2, low, half, kv_half), segment, sink)
            outs.append(o if half == kv_half else pltpu.roll(o, HALF, 1))
        ow_ref[:, j * LANES:(j + 1) * LANES] = jnp.where(low, outs[0], outs[1]).astype(BF16)

    for j in range(NA_HEADS // 2):
        q2 = tile(qn_ref, j) * ATT_SCALE
        segment = [(tile(kn_ref, j).astype(BF16), tile(vn_ref, j).astype(BF16), None)]
        outs = [_attend(_one_head_query(q2, low, half, half), segment) for half in range(2)]
        on_ref[:, j * LANES:(j + 1) * LANES] = jnp.where(low, outs[0], outs[1]).astype(BF16)


def _ctx_attention(proj, win_sink, layer):
    w768 = lambda off: pl.BlockSpec((SEQ, 768), lambda b: (b, off // 768))
    w256 = lambda off: pl.BlockSpec((SEQ, WIN_KV), lambda b: (b, off // WIN_KV))
    out_spec = pl.BlockSpec((SEQ, 768), lambda b: (b, 0))
    return pl.pallas_call(
        functools.partial(_ctx_attn_kernel, layer=layer),
        grid=(BATCH,),
        in_specs=[pl.BlockSpec(memory_space=pltpu.SMEM),
                  w768(QW_OFF), w256(KW_OFF), w256(VW_OFF), w768(QN_OFF), w768(KN_OFF), w768(VN_OFF)],
        out_specs=[out_spec, out_spec],
        out_shape=[jax.ShapeDtypeStruct((N_TOK, 768), BF16)] * 2,
        compiler_params=_cparams(("parallel",), 32),
        name="ctx_attention",
    )(win_sink, proj, proj, proj, proj, proj, proj)


def _rope_tables():
    nf = HEAD_DIM // 4
    pos = np.arange(DEC_SEQ)
    row = (pos // GRID_W).astype(np.float32)
    col = (pos % GRID_W).astype(np.float32)
    inv = (np.float32(ROPE_BASE) ** (-np.arange(nf, dtype=np.float32) / np.float32(nf))).astype(np.float32)
    ang_row = (row[:, None] * inv[None, :]).astype(np.float32)
    ang_col = (col[:, None] * inv[None, :]).astype(np.float32)
    cos_h = np.concatenate([np.cos(ang_row), np.cos(ang_row), np.cos(ang_col), np.cos(ang_col)], axis=-1)
    sin_h = np.concatenate([-np.sin(ang_row), np.sin(ang_row), -np.sin(ang_col), np.sin(ang_col)], axis=-1)
    cos_q = np.tile(cos_h, (1, WIN_HEADS)).astype(np.float32)
    sin_q = np.tile(sin_h, (1, WIN_HEADS)).astype(np.float32)
    return cos_q, sin_q


def _rope(x, cos, sin_signed):
    width = x.shape[-1]
    q = HEAD_DIM // 4
    lane = lax.broadcasted_iota(jnp.int32, x.shape, 1)
    partner = jnp.where((lane & (2 * q - 1)) < q, pltpu.roll(x, width - q, 1), pltpu.roll(x, q, 1))
    return x * cos + partner * sin_signed


def _win_attn_kernel(sink_ref, q_ref, kp_ref, kc_ref, kn_ref, vp_ref, vc_ref, vn_ref, ck_ref, cv_ref,
                     cosq_ref, sinq_ref, cosp_ref, sinp_ref, cosc_ref, sinc_ref, cosn_ref, sinn_ref,
                     o_prev_ref, o_ref, *, layer):
    del o_prev_ref
    n = pl.program_id(1)
    nb = DEC_SEQ // WIN_BLOCK
    low = _lane_is_low(WIN_BLOCK)

    q = _rope(q_ref[...], cosq_ref[...], sinq_ref[...]) * ATT_SCALE
    k_loc = jnp.concatenate([_rope(kp_ref[...], cosp_ref[...], sinp_ref[...]),
                             _rope(kc_ref[...], cosc_ref[...], sinc_ref[...]),
                             _rope(kn_ref[...], cosn_ref[...], sinn_ref[...])], axis=0).astype(BF16)
    v_loc = jnp.concatenate([vp_ref[...], vc_ref[...], vn_ref[...]], axis=0).astype(BF16)
    k_ctx = ck_ref[...].astype(BF16)
    v_ctx = cv_ref[...].astype(BF16)

    qi = lax.broadcasted_iota(jnp.int32, (WIN_BLOCK, 3 * WIN_BLOCK), 0)
    kj = lax.broadcasted_iota(jnp.int32, (WIN_BLOCK, 3 * WIN_BLOCK), 1)
    rel = kj - WIN_BLOCK - qi
    valid = jnp.logical_and(rel >= -WINDOW, rel <= WINDOW)
    valid = jnp.logical_and(valid, kj >= jnp.where(n > 0, 0, WIN_BLOCK))
    valid = jnp.logical_and(valid, kj < jnp.where(n < nb - 1, 3 * WIN_BLOCK, 2 * WIN_BLOCK))

    valid = jnp.concatenate([valid] * WIN_GROUP, axis=0)
    band = lambda s: jnp.where(valid, s, NEG_INF)

    def segments_of(kt):
        cols = slice(kt * LANES, (kt + 1) * LANES)
        return [(k_loc[:, cols], v_loc[:, cols], band), (k_ctx[:, cols], v_ctx[:, cols], None)]

    o_ref[...] = _gqa_heads(lambda j: q[:, j * LANES:(j + 1) * LANES], segments_of, sink_ref, layer, low, WIN_BLOCK)


def _win_attention(proj, cache_k, cache_v, win_sink, tables, o_ctx, layer):
    nb = DEC_SEQ // WIN_BLOCK
    row0 = N_CTX // WIN_BLOCK
    cos_q, sin_q, cos_k, sin_k = tables
    prev = lambda n: jnp.maximum(n - 1, 0)
    cur = lambda n: n
    nxt = lambda n: jnp.minimum(n + 1, nb - 1)
    kv_spec = lambda off, f: pl.BlockSpec((WIN_BLOCK, WIN_KV), lambda b, n: (row0 + b * nb + f(n), off // WIN_KV))
    tab_q = pl.BlockSpec((WIN_BLOCK, 768), lambda b, n: (n, 0))
    tab_k = lambda f: pl.BlockSpec((WIN_BLOCK, WIN_KV), lambda b, n: (f(n), 0))
    cache_spec = pl.BlockSpec((None, None, PAST_LEN, WIN_KV), lambda b, n: (b, layer, 0, 0))
    return pl.pallas_call(
        functools.partial(_win_attn_kernel, layer=layer),
        grid=(DEC_BATCH, nb),
        in_specs=[pl.BlockSpec(memory_space=pltpu.SMEM),
                  pl.BlockSpec((WIN_BLOCK, 768), lambda b, n: (row0 + b * nb + n, QW_OFF // 768)),
                  kv_spec(KW_OFF, prev), kv_spec(KW_OFF, cur), kv_spec(KW_OFF, nxt),
                  kv_spec(VW_OFF, prev), kv_spec(VW_OFF, cur), kv_spec(VW_OFF, nxt),
                  cache_spec, cache_spec,
                  tab_q, tab_q, tab_k(prev), tab_k(prev), tab_k(cur), tab_k(cur), tab_k(nxt), tab_k(nxt),
                  pl.BlockSpec(memory_space=pl.ANY)],
        out_specs=pl.BlockSpec((WIN_BLOCK, 768), lambda b, n: (row0 + b * nb + n, 0)),
        out_shape=jax.ShapeDtypeStruct((N_TOK, 768), BF16),
        input_output_aliases={18: 0},
        compiler_params=_cparams(("parallel", "parallel"), 32),
        name="win_attention",
    )(win_sink, proj, proj, proj, proj, proj, proj, proj, cache_k, cache_v,
      cos_q, sin_q, cos_k, sin_k, cos_k, sin_k, cos_k, sin_k, o_ctx)


def _na_window_start(r0):
    return min(max(r0 - NA_ROWS // 2, 0), GRID_H - NA_WIN_ROWS)


def _na_bias_kernel(rpb_ref, o_ref):
    head = pl.program_id(0)
    base = head * (RPB_H * RPB_W)
    qc = lax.broadcasted_iota(jnp.int32, (GRID_W, LANES), 0)
    lane = lax.broadcasted_iota(jnp.int32, (GRID_W, LANES), 1)
    kc = lane & (GRID_W - 1)
    low = lane < HALF
    dc_idx = jnp.clip(kc - qc + NA_COLS - 1, 0, RPB_W - 1)
    q_start = jnp.clip(qc - NA_COLS // 2, 0, GRID_W - NA_COLS)
    col_valid = jnp.logical_and(kc >= q_start, kc < q_start + NA_COLS)
    neg = jnp.full((GRID_W, LANES), NEG_INF, F32)

    pair = {}
    for d in range(-NA_ROWS, NA_ROWS):
        acc = jnp.zeros((GRID_W, LANES), F32)
        for dc in range(RPB_W):
            s_lo = rpb_ref[base + (d + NA_ROWS - 1) * RPB_W + dc] if abs(d) < NA_ROWS else 0.0
            s_hi = rpb_ref[base + (d + NA_ROWS) * RPB_W + dc] if abs(d + 1) < NA_ROWS else 0.0
            acc = jnp.where(dc_idx == dc, jnp.where(low, s_lo, s_hi), acc)
        pair[d] = jnp.where(col_valid, acc, NEG_INF)

    for qb in range(NA_QBLOCKS):
        r0 = NA_QROWS * qb
        ws = _na_window_start(r0)
        for qi in range(NA_QROWS):
            qr = r0 + qi
            rs = min(max(qr - NA_ROWS // 2, 0), GRID_H - NA_ROWS)
            for p in range(NA_WIN_ROWS // 2):
                kr = ws + 2 * p
                ok_lo = rs <= kr < rs + NA_ROWS
                ok_hi = rs <= kr + 1 < rs + NA_ROWS
                d = kr - qr
                if not (ok_lo or ok_hi):
                    tile = neg
                else:
                    tile = pair[d]
                    if not ok_lo:
                        tile = jnp.where(low, NEG_INF, tile)
                    if not ok_hi:
                        tile = jnp.where(low, tile, NEG_INF)
                o_ref[qb, qi * GRID_W:(qi + 1) * GRID_W, p * LANES:(p + 1) * LANES] = tile


def _na_bias(rpb_l):
    return pl.pallas_call(
        _na_bias_kernel,
        grid=(NA_HEADS,),
        in_specs=[pl.BlockSpec(memory_space=pltpu.SMEM)],
        out_specs=pl.BlockSpec((None, NA_QBLOCKS, NA_QROWS * GRID_W, NA_WIN_KEYS), lambda h: (h, 0, 0, 0)),
        out_shape=jax.ShapeDtypeStruct((NA_HEADS, NA_QBLOCKS, NA_QROWS * GRID_W, NA_WIN_KEYS), F32),
        compiler_params=_cparams(("parallel",), 32),
        name="na_bias",
    )(rpb_l.reshape(NA_HEADS * RPB_H * RPB_W))


def _na_attn_kernel(q_ref, k_ref, v_ref, ck_ref, cv_ref, bias_ref, o_prev_ref, o_ref):
    del o_prev_ref
    nq = NA_QROWS * GRID_W
    low = _lane_is_low(nq)
    k_all = k_ref[...].astype(BF16)
    v_all = v_ref[...].astype(BF16)
    k_ctx = ck_ref[...].astype(BF16)
    v_ctx = cv_ref[...].astype(BF16)
    blocks = []
    for qb in range(NA_QBLOCKS):
        start = _na_window_start(NA_QROWS * qb) * GRID_W
        k_loc = k_all[start:start + NA_WIN_KEYS]
        v_loc = v_all[start:start + NA_WIN_KEYS]
        bias = jnp.concatenate([bias_ref[0, qb], bias_ref[1, qb]], axis=0)
        q2 = q_ref[qb * nq:(qb + 1) * nq, :] * ATT_SCALE
        blocks.append(_pair_heads(q2, [(k_loc, v_loc, lambda s, bias=bias: s + bias), (k_ctx, v_ctx, None)], low, nq))
    o_ref[...] = jnp.concatenate(blocks, axis=0)


def _na_attention(proj, cache_k, cache_v, bias, o_ctx, layer):
    nq = NA_QROWS * GRID_W
    seq_blk0 = N_CTX // DEC_SEQ
    seq_spec = lambda off: pl.BlockSpec((DEC_SEQ, LANES), lambda hp, b: (seq_blk0 + b, off // LANES + hp))
    cache_spec = pl.BlockSpec((None, None, PAST_LEN, LANES), lambda hp, b: (b, layer, 0, hp))
    return pl.pallas_call(
        _na_attn_kernel,
        grid=(NA_HEADS // 2, DEC_BATCH),
        in_specs=[seq_spec(QN_OFF), seq_spec(KN_OFF), seq_spec(VN_OFF), cache_spec, cache_spec,
                  pl.BlockSpec((2, NA_QBLOCKS, nq, NA_WIN_KEYS), lambda hp, b: (hp, 0, 0, 0)),
                  pl.BlockSpec(memory_space=pl.ANY)],
        out_specs=pl.BlockSpec((DEC_SEQ, LANES), lambda hp, b: (seq_blk0 + b, hp)),
        out_shape=jax.ShapeDtypeStruct((N_TOK, 768), BF16),
        input_output_aliases={6: 0},
        compiler_params=_cparams(("parallel", "parallel"), 40),
        name="na_attention",
    )(proj, proj, proj, cache_k, cache_v, bias, o_ctx)


def _ssm_prep_kernel(lr_ref, li_ref, ldt_ref, br_ref, bi_ref, lbr_ref, lbi_ref, bbr_ref, bbi_ref):
    lr, li = lr_ref[...], li_ref[...]
    dt = jnp.exp(ldt_ref[...])
    mag = jnp.exp(lr * dt)
    ang = li * dt
    lbr = mag * jnp.cos(ang)
    lbi = mag * jnp.sin(ang)
    nr = lbr - 1.0
    den = lr * lr + li * li
    cr = (nr * lr + lbi * li) / den
    ci = (lbi * lr - nr * li) / den
    br, bi = br_ref[...], bi_ref[...]
    lbr_ref[...] = lbr
    lbi_ref[...] = lbi
    bbr_ref[...] = cr * br - ci * bi
    bbi_ref[...] = cr * bi + ci * br


def _ssm_prep(lam_re, lam_im, log_dt, b_re, b_im):
    shape5 = (DEPTH, 2, SSM_GROUPS, SSM_GROUP, SSM_STATE)
    rows = DEPTH * 2 * SSM_GROUPS * SSM_GROUP
    bc = lambda t: jnp.broadcast_to(t, shape5).reshape(rows, SSM_STATE)
    args = (bc(lam_re[:, :, :, None, :]), bc(lam_im[:, :, :, None, :]), bc(log_dt[:, :, :, None, None]),
            jnp.swapaxes(b_re, -1, -2).reshape(rows, SSM_STATE), jnp.swapaxes(b_im, -1, -2).reshape(rows, SSM_STATE))
    outs = pl.pallas_call(
        _ssm_prep_kernel,
        out_shape=[jax.ShapeDtypeStruct((rows, SSM_STATE), F32)] * 4,
        compiler_params=_cparams(None, 32),
        name="ssm_prep",
    )(*args)
    lbr, lbi, bbr, bbi = [o.reshape(shape5) for o in outs]
    return lbr[:, :, :, 0], lbi[:, :, :, 0], bbr, bbi


def _scan_permutation():
    p = np.zeros((SCAN_ROWS, SCAN_ROWS), np.float32)
    for t in range(SCAN_T):
        for s in range(SUBLANES):
            src_t = t if s < SUBLANES // 2 else SCAN_T - 1 - t
            p[t * SUBLANES + s, s * SCAN_T + src_t] = 1.0
    return p


def _ssm_scan_kernel(*refs, aliased):
    uf_ref, ub_ref, h0_ref, lam_ref, perm_ref, permt_ref, b_ref, c_ref = refs[:8]
    yf_ref, yb_ref, fin_ref, st_ref, bu_ref, y_ref = refs[8 + (2 if aliased else 0):]
    half_rows = SCAN_ROWS // 2

    @pl.when(pl.program_id(1) == 0)
    def _():
        st_ref[...] = h0_ref[...]

    u_bm = jnp.concatenate([uf_ref[...].reshape(half_rows, SSM_WIDTH),
                            ub_ref[...].reshape(half_rows, SSM_WIDTH)], axis=0).astype(BF16)
    u_tm = jnp.dot(perm_ref[...], u_bm, preferred_element_type=F32).astype(BF16)

    fwd = (lax.broadcasted_iota(jnp.int32, (SCAN_ROWS, 1), 0) & (SUBLANES - 1)) < SUBLANES // 2
    zero = jnp.zeros((SCAN_ROWS, LANES), BF16)
    for kb in range(SSM_KB):
        ub = u_tm[:, kb * LANES:(kb + 1) * LANES]
        lhs = jnp.concatenate([jnp.where(fwd, ub, zero), jnp.where(fwd, zero, ub)], axis=1)
        bu_ref[...] = jnp.dot(lhs, b_ref[kb], preferred_element_type=F32)
        ar = lam_ref[kb, 0]
        ai = lam_ref[kb, 1]

        def step(t, carry):
            xr, xi = carry
            r = pl.multiple_of(t * SUBLANES, SUBLANES)
            nr = ar * xr - ai * xi + bu_ref[pl.ds(r, SUBLANES), 0:STATE_COLS]
            ni = ar * xi + ai * xr + bu_ref[pl.ds(r, SUBLANES), STATE_COLS:2 * STATE_COLS]
            bu_ref[pl.ds(r, SUBLANES), 0:STATE_COLS] = nr
            bu_ref[pl.ds(r, SUBLANES), STATE_COLS:2 * STATE_COLS] = ni
            return nr, ni

        xr, xi = lax.fori_loop(0, SCAN_T, step, (st_ref[kb, 0], st_ref[kb, 1]))
        st_ref[kb, 0] = xr
        st_ref[kb, 1] = xi
        xs = bu_ref[...].astype(BF16)
        yy = jnp.dot(xs, c_ref[kb], preferred_element_type=F32)
        y_ref[:, kb * LANES:(kb + 1) * LANES] = jnp.where(fwd, yy[:, :LANES], yy[:, LANES:])
    fin_ref[...] = st_ref[...]

    y = y_ref[...]
    hi = y.astype(BF16)
    rest = y - hi.astype(F32)
    mid = rest.astype(BF16)
    low = (rest - mid.astype(F32)).astype(BF16)
    permt = permt_ref[...]
    y_bm = (jnp.dot(permt, hi, preferred_element_type=F32) + jnp.dot(permt, mid, preferred_element_type=F32)
            + jnp.dot(permt, low, preferred_element_type=F32))
    yf_ref[...] = y_bm[:half_rows].reshape(SUBLANES // 2, SCAN_T, SSM_WIDTH)
    yb_ref[...] = y_bm[half_rows:].reshape(SUBLANES // 2, SCAN_T, SSM_WIDTH)


def _ssm_scan(proj, seq_len, first_seq, n_seq, h0, lam, bmat, cmat, y_prev=None):
    n_all = N_TOK // seq_len
    nc = seq_len // SCAN_T
    blk0 = first_seq // 4
    proj3 = proj.reshape(n_all, seq_len, N_PROJ)
    state_shape = (SSM_KB, 2, SUBLANES, STATE_COLS)
    whole = lambda shape: pl.BlockSpec(shape, lambda b, c: (0,) * len(shape))
    state_spec = pl.BlockSpec((None,) + state_shape, lambda b, c: (b, 0, 0, 0, 0))
    u_spec = lambda chunk: pl.BlockSpec((4, SCAN_T, SSM_WIDTH), lambda b, c: (blk0 + b, chunk(c), U_OFF // SSM_WIDTH))
    y_spec = lambda chunk: pl.BlockSpec((4, SCAN_T, SSM_WIDTH), lambda b, c: (blk0 + b, chunk(c), 0))
    ascending = lambda c: c
    descending = lambda c: nc - 1 - c
    perm = _scan_permutation()
    aliased = y_prev is not None
    extra_specs = [pl.BlockSpec(memory_space=pl.ANY)] * 2 if aliased else []
    extra_args = [t.reshape(n_all, seq_len, SSM_WIDTH) for t in y_prev] if aliased else []
    yf, yb, fin = pl.pallas_call(
        functools.partial(_ssm_scan_kernel, aliased=aliased),
        grid=(n_seq // 4, nc),
        in_specs=[u_spec(ascending), u_spec(descending), state_spec, whole(state_shape),
                  whole((SCAN_ROWS, SCAN_ROWS)), whole((SCAN_ROWS, SCAN_ROWS)),
                  whole((SSM_KB, 2 * LANES, 2 * STATE_COLS)), whole((SSM_KB, 2 * STATE_COLS, 2 * LANES))] + extra_specs,
        out_specs=[y_spec(ascending), y_spec(descending), state_spec],
        out_shape=[jax.ShapeDtypeStruct((n_all, seq_len, SSM_WIDTH), F32)] * 2
                  + [jax.ShapeDtypeStruct((n_seq // 4,) + state_shape, F32)],
        scratch_shapes=[pltpu.VMEM(state_shape, F32), pltpu.VMEM((SCAN_ROWS, 2 * STATE_COLS), F32),
                        pltpu.VMEM((SCAN_ROWS, SSM_WIDTH), F32)],
        input_output_aliases={8: 0, 9: 1} if aliased else {},
        compiler_params=_cparams(("parallel", "arbitrary"), 40),
        name="ssm_scan",
    )(proj3, proj3, h0, lam, jnp.asarray(perm, BF16), jnp.asarray(perm.T, BF16), bmat, cmat, *extra_args)
    return yf.reshape(N_TOK, SSM_WIDTH), yb.reshape(N_TOK, SSM_WIDTH), fin


def _ssm_scan_params(lbr, lbi, bbr, bbi, c_re, c_im):
    eye = jnp.eye(GROUPS_PER_KB, dtype=F32)

    def lam_tile(t):
        t = t.reshape(DEPTH, 2, SSM_KB, STATE_COLS).transpose(0, 2, 1, 3)
        return jnp.repeat(t, SUBLANES // 2, axis=2)

    def b_blocks(t):
        t = t.reshape(DEPTH, 2, SSM_KB, GROUPS_PER_KB, SSM_GROUP, SSM_STATE)
        return jnp.einsum('ldkgnp,gh->ldkgnhp', t, eye).reshape(DEPTH, 2, SSM_KB, LANES, STATE_COLS)

    def c_blocks(t):
        t = t.reshape(DEPTH, 2, SSM_KB, GROUPS_PER_KB, SSM_GROUP, SSM_STATE)
        return jnp.einsum('ldkgnp,gh->ldkgphn', t, eye).reshape(DEPTH, 2, SSM_KB, STATE_COLS, LANES)

    lam = jnp.stack([lam_tile(lbr), lam_tile(lbi)], axis=2)
    b = jnp.concatenate([b_blocks(bbr), b_blocks(bbi)], axis=-1)
    bmat = jnp.concatenate([b[:, 0], b[:, 1]], axis=-2).astype(BF16)
    c = jnp.concatenate([c_blocks(c_re), -c_blocks(c_im)], axis=-2)
    cmat = jnp.concatenate([c[:, 0], c[:, 1]], axis=-1).astype(BF16)
    return lam, bmat, cmat


def _ssm_glu_kernel(yf_ref, yb_ref, u_ref, d_ref, w_ref, b_ref, o_ref):
    y = yf_ref[...] + yb_ref[...] + d_ref[...] * u_ref[...]
    y = jax.nn.gelu(y)
    z = jnp.dot(y.astype(BF16), w_ref[...], preferred_element_type=F32) + b_ref[...]
    o_ref[...] = (y * jax.nn.sigmoid(z)).astype(BF16)


def _ssm_glu(yf, yb, proj, d_skip, w_glu, b_glu, layer):
    tm = 512
    row = pl.BlockSpec((tm, SSM_WIDTH), lambda i: (i, 0))
    vec = pl.BlockSpec((None, 1, SSM_WIDTH), lambda i: (layer, 0, 0))
    return pl.pallas_call(
        _ssm_glu_kernel,
        grid=(N_TOK // tm,),
        in_specs=[row, row, pl.BlockSpec((tm, SSM_WIDTH), lambda i: (i, U_OFF // SSM_WIDTH)), vec,
                  pl.BlockSpec((None, SSM_WIDTH, SSM_WIDTH), lambda i: (layer, 0, 0)), vec],
        out_specs=row,
        out_shape=jax.ShapeDtypeStruct((N_TOK, SSM_WIDTH), BF16),
        compiler_params=_cparams(("parallel",), 32),
        name="ssm_glu",
    )(yf, yb, proj, d_skip.reshape(DEPTH, 1, SSM_WIDTH), w_glu, b_glu.reshape(DEPTH, 1, SSM_WIDTH))


def _ssm_mixer(proj, lam, bmat, cmat, h0_lat, d_skip, w_glu, b_glu, layer):
    h0_ctx = jnp.zeros((BATCH // 4, SSM_KB, 2, SUBLANES, STATE_COLS), F32)
    yf, yb, fin = _ssm_scan(proj, SEQ, 0, BATCH, h0_ctx, lam, bmat, cmat)
    yf, yb, _ = _ssm_scan(proj, DEC_SEQ, N_CTX // DEC_SEQ, DEC_BATCH, h0_lat, lam, bmat, cmat, y_prev=(yf, yb))
    return _ssm_glu(yf, yb, proj, d_skip, w_glu, b_glu, layer), fin


def _state_to_tiles(s_re, s_im):
    def one(t):
        t = t.transpose(1, 0, 2, 3).reshape(SUBLANES, SSM_KB, STATE_COLS)
        return t.transpose(1, 0, 2)
    return jnp.stack([one(s_re), one(s_im)], axis=1)[None]


def _tiles_to_state(fin, part):
    t = fin[:, :, part].reshape(BATCH // 4, SSM_KB, 2, 4, GROUPS_PER_KB, SSM_STATE)
    return t.transpose(0, 3, 2, 1, 4, 5).reshape(BATCH, 2, SSM_GROUPS, SSM_STATE)


def kernel(x_prompt, x_sample, cache_win_k, cache_win_v, cache_na_k, cache_na_v, state_ssm_re, state_ssm_im,
           c, c_ctx, norm1_g, norm2_g, w_mod, b_mod, w_in, ssm_lam_re, ssm_lam_im, ssm_log_dt,
           ssm_b_re, ssm_b_im, ssm_c_re, ssm_c_im, ssm_d, w_glu, b_glu, win_sink, na_rpb,
           w_branch, w_out, w_up, conv_w, conv_b, w_down, final_g):
    x = jnp.concatenate([x_prompt.reshape(N_CTX, D_MODEL), x_sample.reshape(N_LAT, D_MODEL)], axis=0)

    w_in_b = jnp.concatenate([w_in[:, :, 4352:].astype(BF16), w_in[:, :, 0:1536].astype(BF16),
                              w_in[:, :, 2048:4352].astype(BF16), w_in[:, :, 1536:2048].astype(BF16),
                              jnp.zeros((DEPTH, D_MODEL, N_PROJ - N_IN), BF16)], axis=-1)
    w_glu_b, w_branch_b = w_glu.astype(BF16), w_branch.astype(BF16)
    w_out_b, w_up_b, w_down_b = w_out.astype(BF16), w_up.astype(BF16), w_down.astype(BF16)

    cond8 = jnp.concatenate([c_ctx[None], c, jnp.zeros((SUBLANES - 1 - DEC_BATCH, D_MODEL), F32)], axis=0)
    mods = _adaln(cond8, w_mod, b_mod)
    mods = mods.reshape(DEPTH, SUBLANES, 6, D_MODEL).transpose(0, 2, 1, 3)[:, :, :, None, :]

    lbr, lbi, bbr, bbi = _ssm_prep(ssm_lam_re, ssm_lam_im, ssm_log_dt, ssm_b_re, ssm_b_im)
    lam, bmat, cmat = _ssm_scan_params(lbr, lbi, bbr, bbi, ssm_c_re, ssm_c_im)

    cos_q, sin_q = _rope_tables()
    tables = (jnp.asarray(cos_q), jnp.asarray(sin_q), jnp.asarray(cos_q[:, :WIN_KV]), jnp.asarray(sin_q[:, :WIN_KV]))
    cwk = cache_win_k.reshape(DEC_BATCH, DEPTH, PAST_LEN, WIN_KV)
    cwv = cache_win_v.reshape(DEC_BATCH, DEPTH, PAST_LEN, WIN_KV)
    cnk = cache_na_k.reshape(DEC_BATCH, DEPTH, PAST_LEN, NA_HEADS * HEAD_DIM)
    cnv = cache_na_v.reshape(DEC_BATCH, DEPTH, PAST_LEN, NA_HEADS * HEAD_DIM)

    new_wk, new_wv, new_nk, new_nv, new_sre, new_sim = [], [], [], [], [], []
    for l in range(DEPTH):
        proj = _norm_mod_matmul(x, norm1_g, mods, w_in_b, l, tm=1024, tn=IN_TN)

        h0_lat = _state_to_tiles(state_ssm_re[:, l], state_ssm_im[:, l])
        o_ssm, fin = _ssm_mixer(proj, lam[l], bmat[l], cmat[l], h0_lat, ssm_d, w_glu_b, b_glu, l)

        o_win, o_na = _ctx_attention(proj, win_sink, l)
        o_win = _win_attention(proj, cwk, cwv, win_sink, tables, o_win, l)
        o_na = _na_attention(proj, cnk, cnv, _na_bias(na_rpb[l]), o_na, l)

        merged = _branch_merge(o_ssm, o_win, o_na, w_branch_b, proj, l)
        x = _matmul_residual(merged, w_out_b, x, mods, l, 2, tm=1024, tn=1024)
        act = _ffn_up(x, norm2_g, mods, w_up_b, conv_w, conv_b, l)
        x = _matmul_residual(act, w_down_b, x, mods, l, 5, tm=1024, tn=512)

        ctx = proj[:N_CTX]
        new_wk.append(ctx[:, KW_OFF:KW_OFF + WIN_KV].reshape(BATCH, SEQ, WIN_KV_HEADS, HEAD_DIM))
        new_wv.append(ctx[:, VW_OFF:VW_OFF + WIN_KV].reshape(BATCH, SEQ, WIN_KV_HEADS, HEAD_DIM))
        new_nk.append(ctx[:, KN_OFF:KN_OFF + 768].reshape(BATCH, SEQ, NA_HEADS, HEAD_DIM))
        new_nv.append(ctx[:, VN_OFF:VN_OFF + 768].reshape(BATCH, SEQ, NA_HEADS, HEAD_DIM))
        new_sre.append(_tiles_to_state(fin, 0))
        new_sim.append(_tiles_to_state(fin, 1))

    y_ctx = _final_norm(x, final_g, 0, N_CTX)
    y_lat = _final_norm(x, final_g, N_CTX, N_LAT)
    return (y_ctx.reshape(BATCH, SEQ, D_MODEL), y_lat.reshape(DEC_BATCH, DEC_SEQ, D_MODEL),
            jnp.stack(new_wk, axis=1), jnp.stack(new_wv, axis=1), jnp.stack(new_nk, axis=1),
            jnp.stack(new_nv, axis=1), jnp.stack(new_sre, axis=1), jnp.stack(new_sim, axis=1))
```

```python
import functools

import numpy as np
import jax
import jax.numpy as jnp
from jax import lax
from jax.experimental import pallas as pl
from jax.experimental.pallas import tpu as pltpu

F32 = jnp.float32
BF16 = jnp.bfloat16

D_MODEL = 2048
BATCH = 16
SEQ = 256
DEPTH = 4
DEC_BATCH = 4
DEC_SEQ = 1024
PAST_LEN = 512
GRID_W = 64
GRID_H = DEC_SEQ // GRID_W
HEAD_DIM = 64
SSM_WIDTH = 768
SSM_GROUP = 16
SSM_GROUPS = SSM_WIDTH // SSM_GROUP
SSM_STATE = 64
WIN_HEADS = 12
WIN_KV_HEADS = 4
WIN_GROUP = WIN_HEADS // WIN_KV_HEADS
WINDOW = 128
WIN_BLOCK = 128
NA_HEADS = 12
NA_ROWS = 8
NA_COLS = 16
BRANCH_W = 768
D_FF = 5632
CONV_W = 3
ROPE_BASE = 10000.0
EPS = 1e-6
NEG_INF = -1e30
ATT_SCALE = HEAD_DIM ** -0.5
WIN_KV = WIN_KV_HEADS * HEAD_DIM

N_CTX = BATCH * SEQ
N_LAT = DEC_BATCH * DEC_SEQ
N_TOK = N_CTX + N_LAT

LANES = 128
SUBLANES = 8
HALF = LANES // 2
MIB = 1024 * 1024

N_GATE = 3 * D_MODEL
U_OFF = 0
QW_OFF = U_OFF + 768
QN_OFF = QW_OFF + 768
KN_OFF = QN_OFF + 768
VN_OFF = KN_OFF + 768
KW_OFF = VN_OFF + 768
VW_OFF = KW_OFF + WIN_KV
IN_TN = 1536
IN_SUB = 256
IN_SUBS = IN_TN // IN_SUB
N_PROJ = -(-(VW_OFF + WIN_KV) // IN_TN) * IN_TN

SSM_KB = SSM_WIDTH // LANES
GROUPS_PER_KB = LANES // SSM_GROUP
STATE_COLS = GROUPS_PER_KB * SSM_STATE
SCAN_T = 64
SCAN_ROWS = SCAN_T * SUBLANES

NA_QROWS = 2
NA_QBLOCKS = GRID_H // NA_QROWS
NA_WIN_ROWS = 10
NA_WIN_KEYS = NA_WIN_ROWS * GRID_W
RPB_H = 2 * NA_ROWS - 1
RPB_W = 2 * NA_COLS - 1


def _cparams(dims, vmem_mib):
    return pltpu.CompilerParams(dimension_semantics=dims, vmem_limit_bytes=int(vmem_mib * MIB))


def _mod_row(i, tm):
    tiles_ctx = N_CTX // tm
    per_batch = DEC_SEQ // tm
    return jnp.where(i < tiles_ctx, 0, 1 + (i - tiles_ctx) // per_batch)


def _mod_spec(layer, which, tm, tn, col_of):
    return pl.BlockSpec((None, None, None, 1, tn),
                        lambda i, j: (layer, which, _mod_row(i, tm), 0, col_of(i, j)))


def _adaln_kernel(c_ref, w_ref, b_ref, o_ref):
    s = jax.nn.silu(c_ref[...]).astype(BF16)
    o_ref[...] = jnp.dot(s, w_ref[...].astype(BF16), preferred_element_type=F32) + b_ref[...]


def _adaln(cond8, w_mod, b_mod):
    tn = 1024
    n = 6 * D_MODEL
    return pl.pallas_call(
        _adaln_kernel,
        grid=(DEPTH, n // tn),
        in_specs=[pl.BlockSpec((SUBLANES, D_MODEL), lambda l, j: (0, 0)),
                  pl.BlockSpec((None, D_MODEL, tn), lambda l, j: (l, 0, j)),
                  pl.BlockSpec((None, 1, tn), lambda l, j: (l, 0, j))],
        out_specs=pl.BlockSpec((None, SUBLANES, tn), lambda l, j: (l, 0, j)),
        out_shape=jax.ShapeDtypeStruct((DEPTH, SUBLANES, n), F32),
        compiler_params=_cparams(("parallel", "parallel"), 40),
        name="adaln",
    )(cond8, w_mod, b_mod.reshape(DEPTH, 1, n))


def _norm_mod(x, g, sh, sc):
    y = x * lax.rsqrt(jnp.mean(x * x, axis=-1, keepdims=True) + EPS)
    y = y * g
    return y * (1.0 + sc) + sh


def _in_proj_source_block(g):
    n_gate, n_uq, n_na, n_kvw = (N_GATE // IN_SUB, 2 * 768 // IN_SUB, 3 * 768 // IN_SUB, 2 * WIN_KV // IN_SUB)
    src_gate, src_na, src_kvw = 4352 // IN_SUB, 2048 // IN_SUB, 1536 // IN_SUB
    b1, b2, b3 = n_gate + n_uq, n_gate + n_uq + n_na, n_gate + n_uq + n_na + n_kvw
    return jnp.where(g < n_gate, src_gate + g,
                     jnp.where(g < b1, g - n_gate,
                               jnp.where(g < b2, src_na + g - b1,
                                         jnp.where(g < b3, src_kvw + g - b2, 0))))


def _in_proj_kernel(*refs):
    x_ref, g_ref, sh_ref, sc_ref = refs[:4]
    w_refs = refs[4:4 + IN_SUBS]
    gate_ref, proj_ref, h_ref = refs[4 + IN_SUBS:]
    j = pl.program_id(1)

    @pl.when(j == 0)
    def _():
        h_ref[...] = _norm_mod(x_ref[...], g_ref[...], sh_ref[...], sc_ref[...]).astype(BF16)

    def sub_blocks():
        for r, w_ref in enumerate(w_refs):
            yield slice(r * IN_SUB, (r + 1) * IN_SUB), jnp.dot(h_ref[...], w_ref[...], preferred_element_type=F32)

    @pl.when(j < N_GATE // IN_TN)
    def _():
        for cols, acc in sub_blocks():
            gate_ref[:, cols] = jax.nn.sigmoid(acc).astype(BF16)

    @pl.when(j >= N_GATE // IN_TN)
    def _():
        for cols, acc in sub_blocks():
            proj_ref[:, cols] = acc


def _in_proj(x, norm_g, mods, w_in_b, layer):
    tm = 1024
    n_gate_tiles = N_GATE // IN_TN
    zero = lambda i, j: 0
    w_spec = lambda r: pl.BlockSpec((None, D_MODEL, IN_SUB),
                                    lambda i, j: (layer, 0, _in_proj_source_block(IN_SUBS * j + r)))
    return pl.pallas_call(
        _in_proj_kernel,
        grid=(N_TOK // tm, (N_GATE + N_PROJ) // IN_TN),
        in_specs=[pl.BlockSpec((tm, D_MODEL), lambda i, j: (i, 0)),
                  pl.BlockSpec((None, 1, D_MODEL), lambda i, j: (layer, 0, 0)),
                  _mod_spec(layer, 0, tm, D_MODEL, zero),
                  _mod_spec(layer, 1, tm, D_MODEL, zero)] + [w_spec(r) for r in range(IN_SUBS)],
        out_specs=[pl.BlockSpec((tm, IN_TN), lambda i, j: (i, jnp.minimum(j, n_gate_tiles - 1))),
                   pl.BlockSpec((tm, IN_TN), lambda i, j: (i, jnp.maximum(j - n_gate_tiles, 0)))],
        out_shape=[jax.ShapeDtypeStruct((N_TOK, N_GATE), BF16), jax.ShapeDtypeStruct((N_TOK, N_PROJ), F32)],
        scratch_shapes=[pltpu.VMEM((tm, D_MODEL), BF16)],
        compiler_params=_cparams(("parallel", "arbitrary"), 60),
        name="in_proj",
    )(x, norm_g.reshape(DEPTH, 1, D_MODEL), mods, mods, *([w_in_b] * IN_SUBS))


def _matmul_residual_kernel(a_ref, w_ref, x_ref, g_ref, o_ref):
    acc = jnp.dot(a_ref[...], w_ref[...], preferred_element_type=F32)
    o_ref[...] = x_ref[...] + g_ref[...] * acc


def _matmul_residual(a, w, x, mods, layer, which, tm, tn):
    k = a.shape[-1]
    return pl.pallas_call(
        _matmul_residual_kernel,
        grid=(N_TOK // tm, D_MODEL // tn),
        in_specs=[pl.BlockSpec((tm, k), lambda i, j: (i, 0)),
                  pl.BlockSpec((None, k, tn), lambda i, j: (layer, 0, j)),
                  pl.BlockSpec((tm, tn), lambda i, j: (i, j)),
                  _mod_spec(layer, which, tm, tn, lambda i, j: j)],
        out_specs=pl.BlockSpec((tm, tn), lambda i, j: (i, j)),
        out_shape=jax.ShapeDtypeStruct((N_TOK, D_MODEL), F32),
        compiler_params=_cparams(("parallel", "arbitrary"), 52),
        name="matmul_residual",
    )(a, w, x, mods)


def _branch_merge_kernel(o0_ref, o1_ref, o2_ref, w0_ref, w1_ref, w2_ref, g0_ref, g1_ref, g2_ref, m_ref):
    m = g0_ref[...].astype(F32) * jnp.dot(o0_ref[...], w0_ref[...], preferred_element_type=F32)
    m = m + g1_ref[...].astype(F32) * jnp.dot(o1_ref[...], w1_ref[...], preferred_element_type=F32)
    m = m + g2_ref[...].astype(F32) * jnp.dot(o2_ref[...], w2_ref[...], preferred_element_type=F32)
    m_ref[...] = m.astype(BF16)


def _branch_merge(o_ssm, o_win, o_na, w_branch, gates, layer):
    tm, tn = 1024, 512
    nj = D_MODEL // tn
    o_spec = pl.BlockSpec((tm, BRANCH_W), lambda i, j: (i, 0))
    w_spec = lambda k: pl.BlockSpec((None, None, BRANCH_W, tn), lambda i, j: (layer, k, 0, j))
    g_spec = lambda k: pl.BlockSpec((tm, tn), lambda i, j: (i, k * nj + j))
    return pl.pallas_call(
        _branch_merge_kernel,
        grid=(N_TOK // tm, nj),
        in_specs=[o_spec, o_spec, o_spec, w_spec(0), w_spec(1), w_spec(2), g_spec(0), g_spec(1), g_spec(2)],
        out_specs=pl.BlockSpec((tm, tn), lambda i, j: (i, j)),
        out_shape=jax.ShapeDtypeStruct((N_TOK, D_MODEL), BF16),
        compiler_params=_cparams(("parallel", "arbitrary"), 48),
        name="branch_merge",
    )(o_ssm, o_win, o_na, w_branch, w_branch, w_branch, gates, gates, gates)


FFN_TM = 1024
FFN_TF = 512


def _ffn_up_kernel(x_ref, g_ref, sh_ref, sc_ref, wa_ref, wb_ref, cwa_ref, cwb_ref, cba_ref, cbb_ref,
                   o_ref, h_ref):
    i = pl.program_id(0)

    @pl.when(pl.program_id(1) == 0)
    def _():
        h_ref[...] = _norm_mod(x_ref[...], g_ref[...], sh_ref[...], sc_ref[...]).astype(BF16)

    seq = jnp.where(i < N_CTX // FFN_TM, SEQ, DEC_SEQ)
    pos = lax.broadcasted_iota(jnp.int32, (FFN_TM, 1), 0) & (seq - 1)
    first = pos == 0
    last = pos == seq - 1

    def conv(u, cw_ref, cb_ref):
        prev = jnp.where(first, 0.0, pltpu.roll(u, 1, 0))
        nxt = jnp.where(last, 0.0, pltpu.roll(u, FFN_TM - 1, 0))
        return prev * cw_ref[0:1, :] + u * cw_ref[1:2, :] + nxt * cw_ref[2:3, :] + cb_ref[...]

    h = h_ref[...]
    a = conv(jnp.dot(h, wa_ref[...], preferred_element_type=F32), cwa_ref, cba_ref)
    b = conv(jnp.dot(h, wb_ref[...], preferred_element_type=F32), cwb_ref, cbb_ref)
    o_ref[...] = (jax.nn.silu(a) * b).astype(BF16)


def _ffn_up(x, norm_g, mods, w_up, conv_w, conv_b, layer):
    tm, tf = FFN_TM, FFN_TF
    nj = D_FF // tf
    zero = lambda i, j: 0
    conv_b3 = conv_b.reshape(DEPTH, 1, 2 * D_FF)
    return pl.pallas_call(
        _ffn_up_kernel,
        grid=(N_TOK // tm, nj),
        in_specs=[pl.BlockSpec((tm, D_MODEL), lambda i, j: (i, 0)),
                  pl.BlockSpec((None, 1, D_MODEL), lambda i, j: (layer, 0, 0)),
                  _mod_spec(layer, 3, tm, D_MODEL, zero),
                  _mod_spec(layer, 4, tm, D_MODEL, zero),
                  pl.BlockSpec((None, D_MODEL, tf), lambda i, j: (layer, 0, j)),
                  pl.BlockSpec((None, D_MODEL, tf), lambda i, j: (layer, 0, nj + j)),
                  pl.BlockSpec((None, CONV_W, tf), lambda i, j: (layer, 0, j)),
                  pl.BlockSpec((None, CONV_W, tf), lambda i, j: (layer, 0, nj + j)),
                  pl.BlockSpec((None, 1, tf), lambda i, j: (layer, 0, j)),
                  pl.BlockSpec((None, 1, tf), lambda i, j: (layer, 0, nj + j))],
        out_specs=pl.BlockSpec((tm, tf), lambda i, j: (i, j)),
        out_shape=jax.ShapeDtypeStruct((N_TOK, D_FF), BF16),
        scratch_shapes=[pltpu.VMEM((tm, D_MODEL), BF16)],
        compiler_params=_cparams(("parallel", "arbitrary"), 56),
        name="ffn_up",
    )(x, norm_g.reshape(DEPTH, 1, D_MODEL), mods, mods, w_up, w_up, conv_w, conv_w, conv_b3, conv_b3)


def _rmsnorm_kernel(x_ref, g_ref, o_ref):
    x = x_ref[...]
    o_ref[...] = x * lax.rsqrt(jnp.mean(x * x, axis=-1, keepdims=True) + EPS) * g_ref[...]


def _final_norm(x, g, row0, rows):
    tm = 512
    return pl.pallas_call(
        _rmsnorm_kernel,
        grid=(rows // tm,),
        in_specs=[pl.BlockSpec((tm, D_MODEL), lambda i: (row0 // tm + i, 0)),
                  pl.BlockSpec((1, D_MODEL), lambda i: (0, 0))],
        out_specs=pl.BlockSpec((tm, D_MODEL), lambda i: (i, 0)),
        out_shape=jax.ShapeDtypeStruct((rows, D_MODEL), F32),
        compiler_params=_cparams(("parallel",), 32),
        name="final_norm",
    )(x, g.reshape(1, D_MODEL))


def _dot_nt(a, b):
    return lax.dot_general(a, b, (((1,), (1,)), ((), ())), preferred_element_type=F32)


def _lane_is_low(rows):
    return lax.broadcasted_iota(jnp.int32, (rows, LANES), 1) < HALF


def _one_head_query(q2, low, half, kv_half):
    qa = jnp.where(low if half == 0 else jnp.logical_not(low), q2, 0.0)
    if half != kv_half:
        qa = pltpu.roll(qa, HALF, 1)
    return qa.astype(BF16)


def _attend(qa, segments, sink=None):
    logits = []
    for k, _, fix in segments:
        s = _dot_nt(qa, k)
        logits.append(s if fix is None else fix(s))
    m = functools.reduce(jnp.maximum, [jnp.max(s, axis=-1, keepdims=True) for s in logits])
    if sink is not None:
        m = jnp.maximum(m, sink)
    es = [jnp.exp(s - m) for s in logits]
    den = functools.reduce(jnp.add, [jnp.sum(e, axis=-1, keepdims=True) for e in es])
    if sink is not None:
        den = den + jnp.exp(sink - m)
    o = functools.reduce(jnp.add, [jnp.dot(e.astype(BF16), v, preferred_element_type=F32)
                                   for e, (_, v, _) in zip(es, segments)])
    return o / den


def _gqa_heads(q_tile, segments_of, sink_ref, layer, low, rows):
    head_out = {}
    for kv in range(WIN_KV_HEADS):
        kt, kv_half = kv // 2, kv % 2
        heads = [kv * WIN_GROUP + g for g in range(WIN_GROUP)]
        qa = jnp.concatenate([_one_head_query(q_tile(h // 2), low, h % 2, kv_half) for h in heads], axis=0)
        sink = jnp.concatenate([jnp.full((rows, 1), sink_ref[layer, h], F32) for h in heads], axis=0)
        o = _attend(qa, segments_of(kt), sink)
        for g, h in enumerate(heads):
            og = o[g * rows:(g + 1) * rows]
            head_out[h] = og if h % 2 == kv_half else pltpu.roll(og, HALF, 1)
    return jnp.concatenate([jnp.where(low, head_out[2 * j], head_out[2 * j + 1]).astype(BF16)
                            for j in range(WIN_HEADS // 2)], axis=1)


def _pair_heads(q2, segments, low, rows):
    qa = jnp.concatenate([_one_head_query(q2, low, 0, 0), _one_head_query(q2, low, 1, 1)], axis=0)
    o = _attend(qa, segments)
    return jnp.where(low, o[:rows], o[rows:]).astype(BF16)


def _ctx_attn_kernel(sink_ref, qw_ref, kw_ref, vw_ref, qn_ref, kn_ref, vn_ref, ow_ref, on_ref, *, layer):
    low = _lane_is_low(SEQ)
    tile = lambda ref, j: ref[:, j * LANES:(j + 1) * LANES]

    for j in range(WIN_HEADS // 2):
        q2 = tile(qw_ref, j) * ATT_SCALE
        outs = []
        for half in range(2):
            head = 2 * j + half
            kv = head // WIN_GROUP
            kt, kv_half = kv // 2, kv % 2
            segment = [(tile(kw_ref, kt).astype(BF16), tile(vw_ref, kt).astype(BF16), None)]
            sink = jnp.full((SEQ, 1), sink_ref[layer, head], F32)
            o = _attend(_one_head_query(q2, low, half, kv_half), segment, sink)
            outs.append(o if half == kv_half else pltpu.roll(o, HALF, 1))
        ow_ref[:, j * LANES:(j + 1) * LANES] = jnp.where(low, outs[0], outs[1]).astype(BF16)

    for j in range(NA_HEADS // 2):
        q2 = tile(qn_ref, j) * ATT_SCALE
        segment = [(tile(kn_ref, j).astype(BF16), tile(vn_ref, j).astype(BF16), None)]
        outs = [_attend(_one_head_query(q2, low, half, half), segment) for half in range(2)]
        on_ref[:, j * LANES:(j + 1) * LANES] = jnp.where(low, outs[0], outs[1]).astype(BF16)


def _ctx_attention(proj, win_sink, layer):
    w768 = lambda off: pl.BlockSpec((SEQ, 768), lambda b: (b, off // 768))
    w256 = lambda off: pl.BlockSpec((SEQ, WIN_KV), lambda b: (b, off // WIN_KV))
    out_spec = pl.BlockSpec((SEQ, 768), lambda b: (b, 0))
    return pl.pallas_call(
        functools.partial(_ctx_attn_kernel, layer=layer),
        grid=(BATCH,),
        in_specs=[pl.BlockSpec(memory_space=pltpu.SMEM),
                  w768(QW_OFF), w256(KW_OFF), w256(VW_OFF), w768(QN_OFF), w768(KN_OFF), w768(VN_OFF)],
        out_specs=[out_spec, out_spec],
        out_shape=[jax.ShapeDtypeStruct((N_TOK, 768), BF16)] * 2,
        compiler_params=_cparams(("parallel",), 32),
        name="ctx_attention",
    )(win_sink, proj, proj, proj, proj, proj, proj)


def _rope_tables():
    nf = HEAD_DIM // 4
    pos = np.arange(DEC_SEQ)
    row = (pos // GRID_W).astype(np.float32)
    col = (pos % GRID_W).astype(np.float32)
    inv = (np.float32(ROPE_BASE) ** (-np.arange(nf, dtype=np.float32) / np.float32(nf))).astype(np.float32)
    ang_row = (row[:, None] * inv[None, :]).astype(np.float32)
    ang_col = (col[:, None] * inv[None, :]).astype(np.float32)
    cos_h = np.concatenate([np.cos(ang_row), np.cos(ang_row), np.cos(ang_col), np.cos(ang_col)], axis=-1)
    sin_h = np.concatenate([-np.sin(ang_row), np.sin(ang_row), -np.sin(ang_col), np.sin(ang_col)], axis=-1)
    cos_q = np.tile(cos_h, (1, WIN_HEADS)).astype(np.float32)
    sin_q = np.tile(sin_h, (1, WIN_HEADS)).astype(np.float32)
    return cos_q, sin_q


def _rope(x, cos, sin_signed):
    width = x.shape[-1]
    q = HEAD_DIM // 4
    lane = lax.broadcasted_iota(jnp.int32, x.shape, 1)
    partner = jnp.where((lane & (2 * q - 1)) < q, pltpu.roll(x, width - q, 1), pltpu.roll(x, q, 1))
    return x * cos + partner * sin_signed


def _win_attn_kernel(sink_ref, q_ref, kp_ref, kc_ref, kn_ref, vp_ref, vc_ref, vn_ref, ck_ref, cv_ref,
                     cosq_ref, sinq_ref, cosp_ref, sinp_ref, cosc_ref, sinc_ref, cosn_ref, sinn_ref,
                     o_prev_ref, o_ref, *, layer):
    del o_prev_ref
    n = pl.program_id(1)
    nb = DEC_SEQ // WIN_BLOCK
    low = _lane_is_low(WIN_BLOCK)

    q = _rope(q_ref[...], cosq_ref[...], sinq_ref[...]) * ATT_SCALE
    k_loc = jnp.concatenate([_rope(kp_ref[...], cosp_ref[...], sinp_ref[...]),
                             _rope(kc_ref[...], cosc_ref[...], sinc_ref[...]),
                             _rope(kn_ref[...], cosn_ref[...], sinn_ref[...])], axis=0).astype(BF16)
    v_loc = jnp.concatenate([vp_ref[...], vc_ref[...], vn_ref[...]], axis=0).astype(BF16)
    k_ctx = ck_ref[...].astype(BF16)
    v_ctx = cv_ref[...].astype(BF16)

    qi = lax.broadcasted_iota(jnp.int32, (WIN_BLOCK, 3 * WIN_BLOCK), 0)
    kj = lax.broadcasted_iota(jnp.int32, (WIN_BLOCK, 3 * WIN_BLOCK), 1)
    rel = kj - WIN_BLOCK - qi
    valid = jnp.logical_and(rel >= -WINDOW, rel <= WINDOW)
    valid = jnp.logical_and(valid, kj >= jnp.where(n > 0, 0, WIN_BLOCK))
    valid = jnp.logical_and(valid, kj < jnp.where(n < nb - 1, 3 * WIN_BLOCK, 2 * WIN_BLOCK))

    valid = jnp.concatenate([valid] * WIN_GROUP, axis=0)
    band = lambda s: jnp.where(valid, s, NEG_INF)

    def segments_of(kt):
        cols = slice(kt * LANES, (kt + 1) * LANES)
        return [(k_loc[:, cols], v_loc[:, cols], band), (k_ctx[:, cols], v_ctx[:, cols], None)]

    o_ref[...] = _gqa_heads(lambda j: q[:, j * LANES:(j + 1) * LANES], segments_of, sink_ref, layer, low, WIN_BLOCK)


def _win_attention(proj, cache_k, cache_v, win_sink, tables, o_ctx, layer):
    nb = DEC_SEQ // WIN_BLOCK
    row0 = N_CTX // WIN_BLOCK
    cos_q, sin_q, cos_k, sin_k = tables
    prev = lambda n: jnp.maximum(n - 1, 0)
    cur = lambda n: n
    nxt = lambda n: jnp.minimum(n + 1, nb - 1)
    kv_spec = lambda off, f: pl.BlockSpec((WIN_BLOCK, WIN_KV), lambda b, n: (row0 + b * nb + f(n), off // WIN_KV))
    tab_q = pl.BlockSpec((WIN_BLOCK, 768), lambda b, n: (n, 0))
    tab_k = lambda f: pl.BlockSpec((WIN_BLOCK, WIN_KV), lambda b, n: (f(n), 0))
    cache_spec = pl.BlockSpec((None, None, PAST_LEN, WIN_KV), lambda b, n: (b, layer, 0, 0))
    return pl.pallas_call(
        functools.partial(_win_attn_kernel, layer=layer),
        grid=(DEC_BATCH, nb),
        in_specs=[pl.BlockSpec(memory_space=pltpu.SMEM),
                  pl.BlockSpec((WIN_BLOCK, 768), lambda b, n: (row0 + b * nb + n, QW_OFF // 768)),
                  kv_spec(KW_OFF, prev), kv_spec(KW_OFF, cur), kv_spec(KW_OFF, nxt),
                  kv_spec(VW_OFF, prev), kv_spec(VW_OFF, cur), kv_spec(VW_OFF, nxt),
                  cache_spec, cache_spec,
                  tab_q, tab_q, tab_k(prev), tab_k(prev), tab_k(cur), tab_k(cur), tab_k(nxt), tab_k(nxt),
                  pl.BlockSpec(memory_space=pl.ANY)],
        out_specs=pl.BlockSpec((WIN_BLOCK, 768), lambda b, n: (row0 + b * nb + n, 0)),
        out_shape=jax.ShapeDtypeStruct((N_TOK, 768), BF16),
        input_output_aliases={18: 0},
        compiler_params=_cparams(("parallel", "parallel"), 32),
        name="win_attention",
    )(win_sink, proj, proj, proj, proj, proj, proj, proj, cache_k, cache_v,
      cos_q, sin_q, cos_k, sin_k, cos_k, sin_k, cos_k, sin_k, o_ctx)


def _na_window_start(r0):
    return min(max(r0 - NA_ROWS // 2, 0), GRID_H - NA_WIN_ROWS)


def _na_bias_kernel(rpb_ref, o_ref):
    head = pl.program_id(0)
    base = head * (RPB_H * RPB_W)
    qc = lax.broadcasted_iota(jnp.int32, (GRID_W, LANES), 0)
    lane = lax.broadcasted_iota(jnp.int32, (GRID_W, LANES), 1)
    kc = lane & (GRID_W - 1)
    low = lane < HALF
    dc_idx = jnp.clip(kc - qc + NA_COLS - 1, 0, RPB_W - 1)
    q_start = jnp.clip(qc - NA_COLS // 2, 0, GRID_W - NA_COLS)
    col_valid = jnp.logical_and(kc >= q_start, kc < q_start + NA_COLS)
    neg = jnp.full((GRID_W, LANES), NEG_INF, F32)

    pair = {}
    for d in range(-NA_ROWS, NA_ROWS):
        acc = jnp.zeros((GRID_W, LANES), F32)
        for dc in range(RPB_W):
            s_lo = rpb_ref[base + (d + NA_ROWS - 1) * RPB_W + dc] if abs(d) < NA_ROWS else 0.0
            s_hi = rpb_ref[base + (d + NA_ROWS) * RPB_W + dc] if abs(d + 1) < NA_ROWS else 0.0
            acc = jnp.where(dc_idx == dc, jnp.where(low, s_lo, s_hi), acc)
        pair[d] = jnp.where(col_valid, acc, NEG_INF)

    for qb in range(NA_QBLOCKS):
        r0 = NA_QROWS * qb
        ws = _na_window_start(r0)
        for qi in range(NA_QROWS):
            qr = r0 + qi
            rs = min(max(qr - NA_ROWS // 2, 0), GRID_H - NA_ROWS)
            for p in range(NA_WIN_ROWS // 2):
                kr = ws + 2 * p
                ok_lo = rs <= kr < rs + NA_ROWS
                ok_hi = rs <= kr + 1 < rs + NA_ROWS
                d = kr - qr
                if not (ok_lo or ok_hi):
                    tile = neg
                else:
                    tile = pair[d]
                    if not ok_lo:
                        tile = jnp.where(low, NEG_INF, tile)
                    if not ok_hi:
                        tile = jnp.where(low, tile, NEG_INF)
                o_ref[qb, qi * GRID_W:(qi + 1) * GRID_W, p * LANES:(p + 1) * LANES] = tile


def _na_bias(rpb_l):
    return pl.pallas_call(
        _na_bias_kernel,
        grid=(NA_HEADS,),
        in_specs=[pl.BlockSpec(memory_space=pltpu.SMEM)],
        out_specs=pl.BlockSpec((None, NA_QBLOCKS, NA_QROWS * GRID_W, NA_WIN_KEYS), lambda h: (h, 0, 0, 0)),
        out_shape=jax.ShapeDtypeStruct((NA_HEADS, NA_QBLOCKS, NA_QROWS * GRID_W, NA_WIN_KEYS), F32),
        compiler_params=_cparams(("parallel",), 32),
        name="na_bias",
    )(rpb_l.reshape(NA_HEADS * RPB_H * RPB_W))


def _na_attn_kernel(q_ref, k_ref, v_ref, ck_ref, cv_ref, bias_ref, o_prev_ref, o_ref):
    del o_prev_ref
    nq = NA_QROWS * GRID_W
    low = _lane_is_low(nq)
    k_all = k_ref[...].astype(BF16)
    v_all = v_ref[...].astype(BF16)
    k_ctx = ck_ref[...].astype(BF16)
    v_ctx = cv_ref[...].astype(BF16)
    blocks = []
    for qb in range(NA_QBLOCKS):
        start = _na_window_start(NA_QROWS * qb) * GRID_W
        k_loc = k_all[start:start + NA_WIN_KEYS]
        v_loc = v_all[start:start + NA_WIN_KEYS]
        bias = jnp.concatenate([bias_ref[0, qb], bias_ref[1, qb]], axis=0)
        q2 = q_ref[qb * nq:(qb + 1) * nq, :] * ATT_SCALE
        blocks.append(_pair_heads(q2, [(k_loc, v_loc, lambda s, bias=bias: s + bias), (k_ctx, v_ctx, None)], low, nq))
    o_ref[...] = jnp.concatenate(blocks, axis=0)


def _na_attention(proj, cache_k, cache_v, bias, o_ctx, layer):
    nq = NA_QROWS * GRID_W
    seq_blk0 = N_CTX // DEC_SEQ
    seq_spec = lambda off: pl.BlockSpec((DEC_SEQ, LANES), lambda hp, b: (seq_blk0 + b, off // LANES + hp))
    cache_spec = pl.BlockSpec((None, None, PAST_LEN, LANES), lambda hp, b: (b, layer, 0, hp))
    return pl.pallas_call(
        _na_attn_kernel,
        grid=(NA_HEADS // 2, DEC_BATCH),
        in_specs=[seq_spec(QN_OFF), seq_spec(KN_OFF), seq_spec(VN_OFF), cache_spec, cache_spec,
                  pl.BlockSpec((2, NA_QBLOCKS, nq, NA_WIN_KEYS), lambda hp, b: (hp, 0, 0, 0)),
                  pl.BlockSpec(memory_space=pl.ANY)],
        out_specs=pl.BlockSpec((DEC_SEQ, LANES), lambda hp, b: (seq_blk0 + b, hp)),
        out_shape=jax.ShapeDtypeStruct((N_TOK, 768), BF16),
        input_output_aliases={6: 0},
        compiler_params=_cparams(("parallel", "parallel"), 40),
        name="na_attention",
    )(proj, proj, proj, cache_k, cache_v, bias, o_ctx)


def _ssm_prep_kernel(lr_ref, li_ref, ldt_ref, br_ref, bi_ref, lbr_ref, lbi_ref, bbr_ref, bbi_ref):
    lr, li = lr_ref[...], li_ref[...]
    dt = jnp.exp(ldt_ref[...])
    mag = jnp.exp(lr * dt)
    ang = li * dt
    lbr = mag * jnp.cos(ang)
    lbi = mag * jnp.sin(ang)
    nr = lbr - 1.0
    den = lr * lr + li * li
    cr = (nr * lr + lbi * li) / den
    ci = (lbi * lr - nr * li) / den
    br, bi = br_ref[...], bi_ref[...]
    lbr_ref[...] = lbr
    lbi_ref[...] = lbi
    bbr_ref[...] = cr * br - ci * bi
    bbi_ref[...] = cr * bi + ci * br


def _ssm_prep(lam_re, lam_im, log_dt, b_re, b_im):
    shape5 = (DEPTH, 2, SSM_GROUPS, SSM_GROUP, SSM_STATE)
    rows = DEPTH * 2 * SSM_GROUPS * SSM_GROUP
    bc = lambda t: jnp.broadcast_to(t, shape5).reshape(rows, SSM_STATE)
    args = (bc(lam_re[:, :, :, None, :]), bc(lam_im[:, :, :, None, :]), bc(log_dt[:, :, :, None, None]),
            jnp.swapaxes(b_re, -1, -2).reshape(rows, SSM_STATE), jnp.swapaxes(b_im, -1, -2).reshape(rows, SSM_STATE))
    outs = pl.pallas_call(
        _ssm_prep_kernel,
        out_shape=[jax.ShapeDtypeStruct((rows, SSM_STATE), F32)] * 4,
        compiler_params=_cparams(None, 32),
        name="ssm_prep",
    )(*args)
    lbr, lbi, bbr, bbi = [o.reshape(shape5) for o in outs]
    return lbr[:, :, :, 0], lbi[:, :, :, 0], bbr, bbi


def _scan_permutation():
    p = np.zeros((SCAN_ROWS, SCAN_ROWS), np.float32)
    for t in range(SCAN_T):
        for s in range(SUBLANES):
            src_t = t if s < SUBLANES // 2 else SCAN_T - 1 - t
            p[t * SUBLANES + s, s * SCAN_T + src_t] = 1.0
    return p


def _ssm_scan_kernel(*refs, aliased):
    uf_ref, ub_ref, h0_ref, lam_ref, perm_ref, permt_ref, b_ref, c_ref = refs[:8]
    yf_ref, yb_ref, fin_ref, st_ref, bu_ref, y_ref = refs[8 + (2 if aliased else 0):]
    half_rows = SCAN_ROWS // 2

    @pl.when(pl.program_id(1) == 0)
    def _():
        st_ref[...] = h0_ref[...]

    u_bm = jnp.concatenate([uf_ref[...].reshape(half_rows, SSM_WIDTH),
                            ub_ref[...].reshape(half_rows, SSM_WIDTH)], axis=0).astype(BF16)
    u_tm = jnp.dot(perm_ref[...], u_bm, preferred_element_type=F32).astype(BF16)

    fwd = (lax.broadcasted_iota(jnp.int32, (SCAN_ROWS, 1), 0) & (SUBLANES - 1)) < SUBLANES // 2
    zero = jnp.zeros((SCAN_ROWS, LANES), BF16)
    for kb in range(SSM_KB):
        ub = u_tm[:, kb * LANES:(kb + 1) * LANES]
        lhs = jnp.concatenate([jnp.where(fwd, ub, zero), jnp.where(fwd, zero, ub)], axis=1)
        bu_ref[...] = jnp.dot(lhs, b_ref[kb], preferred_element_type=F32)
        ar = lam_ref[kb, 0]
        ai = lam_ref[kb, 1]

        def step(t, carry):
            xr, xi = carry
            r = pl.multiple_of(t * SUBLANES, SUBLANES)
            nr = ar * xr - ai * xi + bu_ref[pl.ds(r, SUBLANES), 0:STATE_COLS]
            ni = ar * xi + ai * xr + bu_ref[pl.ds(r, SUBLANES), STATE_COLS:2 * STATE_COLS]
            bu_ref[pl.ds(r, SUBLANES), 0:STATE_COLS] = nr
            bu_ref[pl.ds(r, SUBLANES), STATE_COLS:2 * STATE_COLS] = ni
            return nr, ni

        xr, xi = lax.fori_loop(0, SCAN_T, step, (st_ref[kb, 0], st_ref[kb, 1]))
        st_ref[kb, 0] = xr
        st_ref[kb, 1] = xi
        xs = bu_ref[...].astype(BF16)
        yy = jnp.dot(xs, c_ref[kb], preferred_element_type=F32)
        y_ref[:, kb * LANES:(kb + 1) * LANES] = jnp.where(fwd, yy[:, :LANES], yy[:, LANES:])
    fin_ref[...] = st_ref[...]

    y = y_ref[...]
    hi = y.astype(BF16)
    rest = y - hi.astype(F32)
    mid = rest.astype(BF16)
    low = (rest - mid.astype(F32)).astype(BF16)
    permt = permt_ref[...]
    y_bm = (jnp.dot(permt, hi, preferred_element_type=F32) + jnp.dot(permt, mid, preferred_element_type=F32)
            + jnp.dot(permt, low, preferred_element_type=F32))
    yf_ref[...] = y_bm[:half_rows].reshape(SUBLANES // 2, SCAN_T, SSM_WIDTH)
    yb_ref[...] = y_bm[half_rows:].reshape(SUBLANES // 2, SCAN_T, SSM_WIDTH)


def _ssm_scan(proj, seq_len, first_seq, n_seq, h0, lam, bmat, cmat, y_prev=None):
    n_all = N_TOK // seq_len
    nc = seq_len // SCAN_T
    blk0 = first_seq // 4
    proj3 = proj.reshape(n_all, seq_len, N_PROJ)
    state_shape = (SSM_KB, 2, SUBLANES, STATE_COLS)
    whole = lambda shape: pl.BlockSpec(shape, lambda b, c: (0,) * len(shape))
    state_spec = pl.BlockSpec((None,) + state_shape, lambda b, c: (b, 0, 0, 0, 0))
    u_spec = lambda chunk: pl.BlockSpec((4, SCAN_T, SSM_WIDTH), lambda b, c: (blk0 + b, chunk(c), U_OFF // SSM_WIDTH))
    y_spec = lambda chunk: pl.BlockSpec((4, SCAN_T, SSM_WIDTH), lambda b, c: (blk0 + b, chunk(c), 0))
    ascending = lambda c: c
    descending = lambda c: nc - 1 - c
    perm = _scan_permutation()
    aliased = y_prev is not None
    extra_specs = [pl.BlockSpec(memory_space=pl.ANY)] * 2 if aliased else []
    extra_args = [t.reshape(n_all, seq_len, SSM_WIDTH) for t in y_prev] if aliased else []
    yf, yb, fin = pl.pallas_call(
        functools.partial(_ssm_scan_kernel, aliased=aliased),
        grid=(n_seq // 4, nc),
        in_specs=[u_spec(ascending), u_spec(descending), state_spec, whole(state_shape),
                  whole((SCAN_ROWS, SCAN_ROWS)), whole((SCAN_ROWS, SCAN_ROWS)),
                  whole((SSM_KB, 2 * LANES, 2 * STATE_COLS)), whole((SSM_KB, 2 * STATE_COLS, 2 * LANES))] + extra_specs,
        out_specs=[y_spec(ascending), y_spec(descending), state_spec],
        out_shape=[jax.ShapeDtypeStruct((n_all, seq_len, SSM_WIDTH), F32)] * 2
                  + [jax.ShapeDtypeStruct((n_seq // 4,) + state_shape, F32)],
        scratch_shapes=[pltpu.VMEM(state_shape, F32), pltpu.VMEM((SCAN_ROWS, 2 * STATE_COLS), F32),
                        pltpu.VMEM((SCAN_ROWS, SSM_WIDTH), F32)],
        input_output_aliases={8: 0, 9: 1} if aliased else {},
        compiler_params=_cparams(("parallel", "arbitrary"), 40),
        name="ssm_scan",
    )(proj3, proj3, h0, lam, jnp.asarray(perm, BF16), jnp.asarray(perm.T, BF16), bmat, cmat, *extra_args)
    return yf.reshape(N_TOK, SSM_WIDTH), yb.reshape(N_TOK, SSM_WIDTH), fin


def _ssm_scan_params(lbr, lbi, bbr, bbi, c_re, c_im):
    eye = jnp.eye(GROUPS_PER_KB, dtype=F32)

    def lam_tile(t):
        t = t.reshape(DEPTH, 2, SSM_KB, STATE_COLS).transpose(0, 2, 1, 3)
        return jnp.repeat(t, SUBLANES // 2, axis=2)

    def b_blocks(t):
        t = t.reshape(DEPTH, 2, SSM_KB, GROUPS_PER_KB, SSM_GROUP, SSM_STATE)
        return jnp.einsum('ldkgnp,gh->ldkgnhp', t, eye).reshape(DEPTH, 2, SSM_KB, LANES, STATE_COLS)

    def c_blocks(t):
        t = t.reshape(DEPTH, 2, SSM_KB, GROUPS_PER_KB, SSM_GROUP, SSM_STATE)
        return jnp.einsum('ldkgnp,gh->ldkgphn', t, eye).reshape(DEPTH, 2, SSM_KB, STATE_COLS, LANES)

    lam = jnp.stack([lam_tile(lbr), lam_tile(lbi)], axis=2)
    b = jnp.concatenate([b_blocks(bbr), b_blocks(bbi)], axis=-1)
    bmat = jnp.concatenate([b[:, 0], b[:, 1]], axis=-2).astype(BF16)
    c = jnp.concatenate([c_blocks(c_re), -c_blocks(c_im)], axis=-2)
    cmat = jnp.concatenate([c[:, 0], c[:, 1]], axis=-1).astype(BF16)
    return lam, bmat, cmat


def _ssm_glu_kernel(yf_ref, yb_ref, u_ref, d_ref, w_ref, b_ref, o_ref):
    y = yf_ref[...] + yb_ref[...] + d_ref[...] * u_ref[...]
    y = jax.nn.gelu(y)
    z = jnp.dot(y.astype(BF16), w_ref[...], preferred_element_type=F32) + b_ref[...]
    o_ref[...] = (y * jax.nn.sigmoid(z)).astype(BF16)


def _ssm_glu(yf, yb, proj, d_skip, w_glu, b_glu, layer):
    tm = 512
    row = pl.BlockSpec((tm, SSM_WIDTH), lambda i: (i, 0))
    vec = pl.BlockSpec((None, 1, SSM_WIDTH), lambda i: (layer, 0, 0))
    return pl.pallas_call(
        _ssm_glu_kernel,
        grid=(N_TOK // tm,),
        in_specs=[row, row, pl.BlockSpec((tm, SSM_WIDTH), lambda i: (i, U_OFF // SSM_WIDTH)), vec,
                  pl.BlockSpec((None, SSM_WIDTH, SSM_WIDTH), lambda i: (layer, 0, 0)), vec],
        out_specs=row,
        out_shape=jax.ShapeDtypeStruct((N_TOK, SSM_WIDTH), BF16),
        compiler_params=_cparams(("parallel",), 32),
        name="ssm_glu",
    )(yf, yb, proj, d_skip.reshape(DEPTH, 1, SSM_WIDTH), w_glu, b_glu.reshape(DEPTH, 1, SSM_WIDTH))


def _ssm_mixer(proj, lam, bmat, cmat, h0_lat, d_skip, w_glu, b_glu, layer):
    h0_ctx = jnp.zeros((BATCH // 4, SSM_KB, 2, SUBLANES, STATE_COLS), F32)
    yf, yb, fin = _ssm_scan(proj, SEQ, 0, BATCH, h0_ctx, lam, bmat, cmat)
    yf, yb, _ = _ssm_scan(proj, DEC_SEQ, N_CTX // DEC_SEQ, DEC_BATCH, h0_lat, lam, bmat, cmat, y_prev=(yf, yb))
    return _ssm_glu(yf, yb, proj, d_skip, w_glu, b_glu, layer), fin


def _state_to_tiles(s_re, s_im):
    def one(t):
        t = t.transpose(1, 0, 2, 3).reshape(SUBLANES, SSM_KB, STATE_COLS)
        return t.transpose(1, 0, 2)
    return jnp.stack([one(s_re), one(s_im)], axis=1)[None]


def _tiles_to_state(fin, part):
    t = fin[:, :, part].reshape(BATCH // 4, SSM_KB, 2, 4, GROUPS_PER_KB, SSM_STATE)
    return t.transpose(0, 3, 2, 1, 4, 5).reshape(BATCH, 2, SSM_GROUPS, SSM_STATE)


def kernel(x_prompt, x_sample, cache_win_k, cache_win_v, cache_na_k, cache_na_v, state_ssm_re, state_ssm_im,
           c, c_ctx, norm1_g, norm2_g, w_mod, b_mod, w_in, ssm_lam_re, ssm_lam_im, ssm_log_dt,
           ssm_b_re, ssm_b_im, ssm_c_re, ssm_c_im, ssm_d, w_glu, b_glu, win_sink, na_rpb,
           w_branch, w_out, w_up, conv_w, conv_b, w_down, final_g):
    x = jnp.concatenate([x_prompt.reshape(N_CTX, D_MODEL), x_sample.reshape(N_LAT, D_MODEL)], axis=0)

    w_in_b, w_glu_b, w_branch_b = w_in.astype(BF16), w_glu.astype(BF16), w_branch.astype(BF16)
    w_out_b, w_up_b, w_down_b = w_out.astype(BF16), w_up.astype(BF16), w_down.astype(BF16)

    cond8 = jnp.concatenate([c_ctx[None], c, jnp.zeros((SUBLANES - 1 - DEC_BATCH, D_MODEL), F32)], axis=0)
    mods = _adaln(cond8, w_mod, b_mod)
    mods = mods.reshape(DEPTH, SUBLANES, 6, D_MODEL).transpose(0, 2, 1, 3)[:, :, :, None, :]

    lbr, lbi, bbr, bbi = _ssm_prep(ssm_lam_re, ssm_lam_im, ssm_log_dt, ssm_b_re, ssm_b_im)
    lam, bmat, cmat = _ssm_scan_params(lbr, lbi, bbr, bbi, ssm_c_re, ssm_c_im)

    cos_q, sin_q = _rope_tables()
    tables = (jnp.asarray(cos_q), jnp.asarray(sin_q), jnp.asarray(cos_q[:, :WIN_KV]), jnp.asarray(sin_q[:, :WIN_KV]))
    cwk = cache_win_k.reshape(DEC_BATCH, DEPTH, PAST_LEN, WIN_KV)
    cwv = cache_win_v.reshape(DEC_BATCH, DEPTH, PAST_LEN, WIN_KV)
    cnk = cache_na_k.reshape(DEC_BATCH, DEPTH, PAST_LEN, NA_HEADS * HEAD_DIM)
    cnv = cache_na_v.reshape(DEC_BATCH, DEPTH, PAST_LEN, NA_HEADS * HEAD_DIM)

    new_wk, new_wv, new_nk, new_nv, new_sre, new_sim = [], [], [], [], [], []
    for l in range(DEPTH):
        gates, proj = _in_proj(x, norm1_g, mods, w_in_b, l)

        h0_lat = _state_to_tiles(state_ssm_re[:, l], state_ssm_im[:, l])
        o_ssm, fin = _ssm_mixer(proj, lam[l], bmat[l], cmat[l], h0_lat, ssm_d, w_glu_b, b_glu, l)

        o_win, o_na = _ctx_attention(proj, win_sink, l)
        o_win = _win_attention(proj, cwk, cwv, win_sink, tables, o_win, l)
        o_na = _na_attention(proj, cnk, cnv, _na_bias(na_rpb[l]), o_na, l)

        merged = _branch_merge(o_ssm, o_win, o_na, w_branch_b, gates, l)
        x = _matmul_residual(merged, w_out_b, x, mods, l, 2, tm=1024, tn=1024)
        act = _ffn_up(x, norm2_g, mods, w_up_b, conv_w, conv_b, l)
        x = _matmul_residual(act, w_down_b, x, mods, l, 5, tm=1024, tn=512)

        ctx = proj[:N_CTX]
        new_wk.append(ctx[:, KW_OFF:KW_OFF + WIN_KV].reshape(BATCH, SEQ, WIN_KV_HEADS, HEAD_DIM))
        new_wv.append(ctx[:, VW_OFF:VW_OFF + WIN_KV].reshape(BATCH, SEQ, WIN_KV_HEADS, HEAD_DIM))
        new_nk.append(ctx[:, KN_OFF:KN_OFF + 768].reshape(BATCH, SEQ, NA_HEADS, HEAD_DIM))
        new_nv.append(ctx[:, VN_OFF:VN_OFF + 768].reshape(BATCH, SEQ, NA_HEADS, HEAD_DIM))
        new_sre.append(_tiles_to_state(fin, 0))
        new_sim.append(_tiles_to_state(fin, 1))

    y_ctx = _final_norm(x, final_g, 0, N_CTX)
    y_lat = _final_norm(x, final_g, N_CTX, N_LAT)
    return (y_ctx.reshape(BATCH, SEQ, D_MODEL), y_lat.reshape(DEC_BATCH, DEC_SEQ, D_MODEL),
            jnp.stack(new_wk, axis=1), jnp.stack(new_wv, axis=1), jnp.stack(new_nk, axis=1),
            jnp.stack(new_nv, axis=1), jnp.stack(new_sre, axis=1), jnp.stack(new_sim, axis=1))
```

```python
import functools

import numpy as np
import jax
import jax.numpy as jnp
from jax import lax
from jax.experimental import pallas as pl
from jax.experimental.pallas import tpu as pltpu

F32 = jnp.float32
BF16 = jnp.bfloat16

D_MODEL = 2048
BATCH = 16
SEQ = 256
DEPTH = 4
DEC_BATCH = 4
DEC_SEQ = 1024
PAST_LEN = 512
GRID_W = 64
GRID_H = DEC_SEQ // GRID_W
HEAD_DIM = 64
SSM_WIDTH = 768
SSM_GROUP = 16
SSM_GROUPS = SSM_WIDTH // SSM_GROUP
SSM_STATE = 64
WIN_HEADS = 12
WIN_KV_HEADS = 4
WIN_GROUP = WIN_HEADS // WIN_KV_HEADS
WINDOW = 128
WIN_BLOCK = 128
NA_HEADS = 12
NA_ROWS = 8
NA_COLS = 16
BRANCH_W = 768
D_FF = 5632
CONV_W = 3
ROPE_BASE = 10000.0
EPS = 1e-6
NEG_INF = -1e30
ATT_SCALE = HEAD_DIM ** -0.5
WIN_KV = WIN_KV_HEADS * HEAD_DIM

N_CTX = BATCH * SEQ
N_LAT = DEC_BATCH * DEC_SEQ
N_TOK = N_CTX + N_LAT

LANES = 128
SUBLANES = 8
HALF = LANES // 2
MIB = 1024 * 1024

N_GATE = 3 * D_MODEL
U_OFF = 0
QW_OFF = U_OFF + 768
QN_OFF = QW_OFF + 768
KN_OFF = QN_OFF + 768
VN_OFF = KN_OFF + 768
KW_OFF = VN_OFF + 768
VW_OFF = KW_OFF + WIN_KV
IN_TN = 1536
IN_SUB = 256
IN_SUBS = IN_TN // IN_SUB
N_PROJ = -(-(VW_OFF + WIN_KV) // IN_TN) * IN_TN

SSM_KB = SSM_WIDTH // LANES
GROUPS_PER_KB = LANES // SSM_GROUP
STATE_COLS = GROUPS_PER_KB * SSM_STATE
SCAN_T = 64
SCAN_ROWS = SCAN_T * SUBLANES

NA_QROWS = 2
NA_QBLOCKS = GRID_H // NA_QROWS
NA_WIN_ROWS = 10
NA_WIN_KEYS = NA_WIN_ROWS * GRID_W
RPB_H = 2 * NA_ROWS - 1
RPB_W = 2 * NA_COLS - 1


def _cparams(dims, vmem_mib):
    return pltpu.CompilerParams(dimension_semantics=dims, vmem_limit_bytes=int(vmem_mib * MIB))


def _mod_row(i, tm):
    tiles_ctx = N_CTX // tm
    per_batch = DEC_SEQ // tm
    return jnp.where(i < tiles_ctx, 0, 1 + (i - tiles_ctx) // per_batch)


def _mod_spec(layer, which, tm, tn, col_of):
    return pl.BlockSpec((None, None, None, 1, tn),
                        lambda i, j: (layer, which, _mod_row(i, tm), 0, col_of(i, j)))


def _adaln_kernel(c_ref, w_ref, b_ref, o_ref):
    s = jax.nn.silu(c_ref[...]).astype(BF16)
    o_ref[...] = jnp.dot(s, w_ref[...].astype(BF16), preferred_element_type=F32) + b_ref[...]


def _adaln(cond8, w_mod, b_mod):
    tn = 1024
    n = 6 * D_MODEL
    return pl.pallas_call(
        _adaln_kernel,
        grid=(DEPTH, n // tn),
        in_specs=[pl.BlockSpec((SUBLANES, D_MODEL), lambda l, j: (0, 0)),
                  pl.BlockSpec((None, D_MODEL, tn), lambda l, j: (l, 0, j)),
                  pl.BlockSpec((None, 1, tn), lambda l, j: (l, 0, j))],
        out_specs=pl.BlockSpec((None, SUBLANES, tn), lambda l, j: (l, 0, j)),
        out_shape=jax.ShapeDtypeStruct((DEPTH, SUBLANES, n), F32),
        compiler_params=_cparams(("parallel", "parallel"), 40),
        name="adaln",
    )(cond8, w_mod, b_mod.reshape(DEPTH, 1, n))


def _norm_mod(x, g, sh, sc):
    y = x * lax.rsqrt(jnp.mean(x * x, axis=-1, keepdims=True) + EPS)
    y = y * g
    return y * (1.0 + sc) + sh


def _in_proj_source_block(g):
    n_gate, n_uq, n_na, n_kvw = (N_GATE // IN_SUB, 2 * 768 // IN_SUB, 3 * 768 // IN_SUB, 2 * WIN_KV // IN_SUB)
    src_gate, src_na, src_kvw = 4352 // IN_SUB, 2048 // IN_SUB, 1536 // IN_SUB
    b1, b2, b3 = n_gate + n_uq, n_gate + n_uq + n_na, n_gate + n_uq + n_na + n_kvw
    return jnp.where(g < n_gate, src_gate + g,
                     jnp.where(g < b1, g - n_gate,
                               jnp.where(g < b2, src_na + g - b1,
                                         jnp.where(g < b3, src_kvw + g - b2, 0))))


def _in_proj_kernel(*refs):
    x_ref, g_ref, sh_ref, sc_ref = refs[:4]
    w_refs = refs[4:4 + IN_SUBS]
    gate_ref, proj_ref, h_ref = refs[4 + IN_SUBS:]
    j = pl.program_id(1)

    @pl.when(j == 0)
    def _():
        h_ref[...] = _norm_mod(x_ref[...], g_ref[...], sh_ref[...], sc_ref[...]).astype(BF16)

    def sub_blocks():
        for r, w_ref in enumerate(w_refs):
            yield slice(r * IN_SUB, (r + 1) * IN_SUB), jnp.dot(h_ref[...], w_ref[...], preferred_element_type=F32)

    @pl.when(j < N_GATE // IN_TN)
    def _():
        for cols, acc in sub_blocks():
            gate_ref[:, cols] = (0.5 * jnp.tanh(0.5 * acc) + 0.5).astype(BF16)

    @pl.when(j >= N_GATE // IN_TN)
    def _():
        for cols, acc in sub_blocks():
            proj_ref[:, cols] = acc


def _in_proj(x, norm_g, mods, w_in_b, layer):
    tm = 1024
    n_gate_tiles = N_GATE // IN_TN
    zero = lambda i, j: 0
    w_spec = lambda r: pl.BlockSpec((None, D_MODEL, IN_SUB),
                                    lambda i, j: (layer, 0, _in_proj_source_block(IN_SUBS * j + r)))
    return pl.pallas_call(
        _in_proj_kernel,
        grid=(N_TOK // tm, (N_GATE + N_PROJ) // IN_TN),
        in_specs=[pl.BlockSpec((tm, D_MODEL), lambda i, j: (i, 0)),
                  pl.BlockSpec((None, 1, D_MODEL), lambda i, j: (layer, 0, 0)),
                  _mod_spec(layer, 0, tm, D_MODEL, zero),
                  _mod_spec(layer, 1, tm, D_MODEL, zero)] + [w_spec(r) for r in range(IN_SUBS)],
        out_specs=[pl.BlockSpec((tm, IN_TN), lambda i, j: (i, jnp.minimum(j, n_gate_tiles - 1))),
                   pl.BlockSpec((tm, IN_TN), lambda i, j: (i, jnp.maximum(j - n_gate_tiles, 0)))],
        out_shape=[jax.ShapeDtypeStruct((N_TOK, N_GATE), BF16), jax.ShapeDtypeStruct((N_TOK, N_PROJ), F32)],
        scratch_shapes=[pltpu.VMEM((tm, D_MODEL), BF16)],
        compiler_params=_cparams(("parallel", "arbitrary"), 60),
        name="in_proj",
    )(x, norm_g.reshape(DEPTH, 1, D_MODEL), mods, mods, *([w_in_b] * IN_SUBS))


def _matmul_residual_kernel(a_ref, w_ref, x_ref, g_ref, o_ref):
    acc = jnp.dot(a_ref[...], w_ref[...], preferred_element_type=F32)
    o_ref[...] = x_ref[...] + g_ref[...] * acc


def _matmul_residual(a, w, x, mods, layer, which, tm, tn):
    k = a.shape[-1]
    return pl.pallas_call(
        _matmul_residual_kernel,
        grid=(N_TOK // tm, D_MODEL // tn),
        in_specs=[pl.BlockSpec((tm, k), lambda i, j: (i, 0)),
                  pl.BlockSpec((None, k, tn), lambda i, j: (layer, 0, j)),
                  pl.BlockSpec((tm, tn), lambda i, j: (i, j)),
                  _mod_spec(layer, which, tm, tn, lambda i, j: j)],
        out_specs=pl.BlockSpec((tm, tn), lambda i, j: (i, j)),
        out_shape=jax.ShapeDtypeStruct((N_TOK, D_MODEL), F32),
        compiler_params=_cparams(("parallel", "arbitrary"), 52),
        name="matmul_residual",
    )(a, w, x, mods)


def _out_proj_kernel(m_ref, w_ref, x_ref, g1_ref, ng_ref, sh_ref, sc_ref, xo_ref, h_ref):
    x_new = x_ref[...] + g1_ref[...] * jnp.dot(m_ref[...], w_ref[...], preferred_element_type=F32)
    xo_ref[...] = x_new
    h_ref[...] = _norm_mod(x_new, ng_ref[...], sh_ref[...], sc_ref[...]).astype(BF16)


def _out_proj(merged, w_out, x, mods, norm_g, layer):
    tm = 512
    row = pl.BlockSpec((tm, D_MODEL), lambda i: (i, 0))
    mod = lambda which: pl.BlockSpec((None, None, None, 1, D_MODEL), lambda i: (layer, which, _mod_row(i, tm), 0, 0))
    return pl.pallas_call(
        _out_proj_kernel,
        grid=(N_TOK // tm,),
        in_specs=[row, pl.BlockSpec((None, D_MODEL, D_MODEL), lambda i: (layer, 0, 0)), row,
                  mod(2), pl.BlockSpec((None, 1, D_MODEL), lambda i: (layer, 0, 0)), mod(3), mod(4)],
        out_specs=[row, row],
        out_shape=[jax.ShapeDtypeStruct((N_TOK, D_MODEL), F32), jax.ShapeDtypeStruct((N_TOK, D_MODEL), BF16)],
        compiler_params=_cparams(("parallel",), 52),
        name="out_proj",
    )(merged, w_out, x, mods, norm_g.reshape(DEPTH, 1, D_MODEL), mods, mods)


def _branch_merge_kernel(o0_ref, o1_ref, o2_ref, w0_ref, w1_ref, w2_ref, g0_ref, g1_ref, g2_ref, m_ref):
    m = g0_ref[...].astype(F32) * jnp.dot(o0_ref[...], w0_ref[...], preferred_element_type=F32)
    m = m + g1_ref[...].astype(F32) * jnp.dot(o1_ref[...], w1_ref[...], preferred_element_type=F32)
    m = m + g2_ref[...].astype(F32) * jnp.dot(o2_ref[...], w2_ref[...], preferred_element_type=F32)
    m_ref[...] = m.astype(BF16)


def _branch_merge(o_ssm, o_win, o_na, w_branch, gates, layer):
    tm, tn = 1024, 512
    nj = D_MODEL // tn
    o_spec = pl.BlockSpec((tm, BRANCH_W), lambda i, j: (i, 0))
    w_spec = lambda k: pl.BlockSpec((None, None, BRANCH_W, tn), lambda i, j: (layer, k, 0, j))
    g_spec = lambda k: pl.BlockSpec((tm, tn), lambda i, j: (i, k * nj + j))
    return pl.pallas_call(
        _branch_merge_kernel,
        grid=(N_TOK // tm, nj),
        in_specs=[o_spec, o_spec, o_spec, w_spec(0), w_spec(1), w_spec(2), g_spec(0), g_spec(1), g_spec(2)],
        out_specs=pl.BlockSpec((tm, tn), lambda i, j: (i, j)),
        out_shape=jax.ShapeDtypeStruct((N_TOK, D_MODEL), BF16),
        compiler_params=_cparams(("parallel", "arbitrary"), 48),
        name="branch_merge",
    )(o_ssm, o_win, o_na, w_branch, w_branch, w_branch, gates, gates, gates)


FFN_TM = 1024
FFN_TF = 512


def _ffn_up_kernel(h_ref, wa_ref, wb_ref, cwa_ref, cwb_ref, cba_ref, cbb_ref, o_ref):
    i = pl.program_id(0)
    seq = jnp.where(i < N_CTX // FFN_TM, SEQ, DEC_SEQ)
    pos = lax.broadcasted_iota(jnp.int32, (FFN_TM, 1), 0) & (seq - 1)
    first = pos == 0
    last = pos == seq - 1

    def conv(u, cw_ref, cb_ref):
        prev = jnp.where(first, 0.0, pltpu.roll(u, 1, 0))
        nxt = jnp.where(last, 0.0, pltpu.roll(u, FFN_TM - 1, 0))
        return prev * cw_ref[0:1, :] + u * cw_ref[1:2, :] + nxt * cw_ref[2:3, :] + cb_ref[...]

    h = h_ref[...]
    a = conv(jnp.dot(h, wa_ref[...], preferred_element_type=F32), cwa_ref, cba_ref)
    b = conv(jnp.dot(h, wb_ref[...], preferred_element_type=F32), cwb_ref, cbb_ref)
    o_ref[...] = (jax.nn.silu(a) * b).astype(BF16)


def _ffn_up(h2, w_up, conv_w, conv_b, layer):
    tm, tf = FFN_TM, FFN_TF
    nj = D_FF // tf
    conv_b3 = conv_b.reshape(DEPTH, 1, 2 * D_FF)
    return pl.pallas_call(
        _ffn_up_kernel,
        grid=(N_TOK // tm, nj),
        in_specs=[pl.BlockSpec((tm, D_MODEL), lambda i, j: (i, 0)),
                  pl.BlockSpec((None, D_MODEL, tf), lambda i, j: (layer, 0, j)),
                  pl.BlockSpec((None, D_MODEL, tf), lambda i, j: (layer, 0, nj + j)),
                  pl.BlockSpec((None, CONV_W, tf), lambda i, j: (layer, 0, j)),
                  pl.BlockSpec((None, CONV_W, tf), lambda i, j: (layer, 0, nj + j)),
                  pl.BlockSpec((None, 1, tf), lambda i, j: (layer, 0, j)),
                  pl.BlockSpec((None, 1, tf), lambda i, j: (layer, 0, nj + j))],
        out_specs=pl.BlockSpec((tm, tf), lambda i, j: (i, j)),
        out_shape=jax.ShapeDtypeStruct((N_TOK, D_FF), BF16),
        compiler_params=_cparams(("parallel", "arbitrary"), 48),
        name="ffn_up",
    )(h2, w_up, w_up, conv_w, conv_w, conv_b3, conv_b3)


def _rmsnorm_kernel(x_ref, g_ref, o_ref):
    x = x_ref[...]
    o_ref[...] = x * lax.rsqrt(jnp.mean(x * x, axis=-1, keepdims=True) + EPS) * g_ref[...]


def _final_norm(x, g, row0, rows):
    tm = 512
    return pl.pallas_call(
        _rmsnorm_kernel,
        grid=(rows // tm,),
        in_specs=[pl.BlockSpec((tm, D_MODEL), lambda i: (row0 // tm + i, 0)),
                  pl.BlockSpec((1, D_MODEL), lambda i: (0, 0))],
        out_specs=pl.BlockSpec((tm, D_MODEL), lambda i: (i, 0)),
        out_shape=jax.ShapeDtypeStruct((rows, D_MODEL), F32),
        compiler_params=_cparams(("parallel",), 32),
        name="final_norm",
    )(x, g.reshape(1, D_MODEL))


def _dot_nt(a, b):
    return lax.dot_general(a, b, (((1,), (1,)), ((), ())), preferred_element_type=F32)


def _lane_is_low(rows):
    return lax.broadcasted_iota(jnp.int32, (rows, LANES), 1) < HALF


def _one_head_query(q2, low, half, kv_half):
    qa = jnp.where(low if half == 0 else jnp.logical_not(low), q2, 0.0)
    if half != kv_half:
        qa = pltpu.roll(qa, HALF, 1)
    return qa.astype(BF16)


def _attend(qa, segments, sink=None):
    logits = []
    for k, _, fix in segments:
        s = _dot_nt(qa, k)
        logits.append(s if fix is None else fix(s))
    m = functools.reduce(jnp.maximum, [jnp.max(s, axis=-1, keepdims=True) for s in logits])
    if sink is not None:
        m = jnp.maximum(m, sink)
    es = [jnp.exp(s - m) for s in logits]
    den = functools.reduce(jnp.add, [jnp.sum(e, axis=-1, keepdims=True) for e in es])
    if sink is not None:
        den = den + jnp.exp(sink - m)
    o = functools.reduce(jnp.add, [jnp.dot(e.astype(BF16), v, preferred_element_type=F32)
                                   for e, (_, v, _) in zip(es, segments)])
    return o / den


def _gqa_heads(q_tile, segments_of, sink_ref, layer, low, rows):
    head_out = {}
    for kv in range(WIN_KV_HEADS):
        kt, kv_half = kv // 2, kv % 2
        heads = [kv * WIN_GROUP + g for g in range(WIN_GROUP)]
        qa = jnp.concatenate([_one_head_query(q_tile(h // 2), low, h % 2, kv_half) for h in heads], axis=0)
        sink = jnp.concatenate([jnp.full((rows, 1), sink_ref[layer, h], F32) for h in heads], axis=0)
        o = _attend(qa, segments_of(kt), sink)
        for g, h in enumerate(heads):
            og = o[g * rows:(g + 1) * rows]
            head_out[h] = og if h % 2 == kv_half else pltpu.roll(og, HALF, 1)
    return jnp.concatenate([jnp.where(low, head_out[2 * j], head_out[2 * j + 1]).astype(BF16)
                            for j in range(WIN_HEADS // 2)], axis=1)


def _pair_heads(q2, segments, low, rows):
    qa = jnp.concatenate([_one_head_query(q2, low, 0, 0), _one_head_query(q2, low, 1, 1)], axis=0)
    o = _attend(qa, segments)
    return jnp.where(low, o[:rows], o[rows:]).astype(BF16)


def _ctx_attn_kernel(sink_ref, qw_ref, kw_ref, vw_ref, qn_ref, kn_ref, vn_ref, ow_ref, on_ref, *, layer):
    low = _lane_is_low(SEQ)
    tile = lambda ref, j: ref[:, j * LANES:(j + 1) * LANES]

    for j in range(WIN_HEADS // 2):
        q2 = tile(qw_ref, j) * ATT_SCALE
        outs = []
        for half in range(2):
            head = 2 * j + half
            kv = head // WIN_GROUP
            kt, kv_half = kv // 2, kv % 2
            segment = [(tile(kw_ref, kt).astype(BF16), tile(vw_ref, kt).astype(BF16), None)]
            sink = jnp.full((SEQ, 1), sink_ref[layer, head], F32)
            o = _attend(_one_head_query(q2, low, half, kv_half), segment, sink)
            outs.append(o if half == kv_half else pltpu.roll(o, HALF, 1))
        ow_ref[:, j * LANES:(j + 1) * LANES] = jnp.where(low, outs[0], outs[1]).astype(BF16)

    for j in range(NA_HEADS // 2):
        q2 = tile(qn_ref, j) * ATT_SCALE
        segment = [(tile(kn_ref, j).astype(BF16), tile(vn_ref, j).astype(BF16), None)]
        outs = [_attend(_one_head_query(q2, low, half, half), segment) for half in range(2)]
        on_ref[:, j * LANES:(j + 1) * LANES] = jnp.where(low, outs[0], outs[1]).astype(BF16)


def _ctx_attention(proj, win_sink, layer):
    w768 = lambda off: pl.BlockSpec((SEQ, 768), lambda b: (b, off // 768))
    w256 = lambda off: pl.BlockSpec((SEQ, WIN_KV), lambda b: (b, off // WIN_KV))
    out_spec = pl.BlockSpec((SEQ, 768), lambda b: (b, 0))
    return pl.pallas_call(
        functools.partial(_ctx_attn_kernel, layer=layer),
        grid=(BATCH,),
        in_specs=[pl.BlockSpec(memory_space=pltpu.SMEM),
                  w768(QW_OFF), w256(KW_OFF), w256(VW_OFF), w768(QN_OFF), w768(KN_OFF), w768(VN_OFF)],
        out_specs=[out_spec, out_spec],
        out_shape=[jax.ShapeDtypeStruct((N_TOK, 768), BF16)] * 2,
        compiler_params=_cparams(("parallel",), 32),
        name="ctx_attention",
    )(win_sink, proj, proj, proj, proj, proj, proj)


def _rope_tables():
    nf = HEAD_DIM // 4
    pos = np.arange(DEC_SEQ)
    row = (pos // GRID_W).astype(np.float32)
    col = (pos % GRID_W).astype(np.float32)
    inv = (np.float32(ROPE_BASE) ** (-np.arange(nf, dtype=np.float32) / np.float32(nf))).astype(np.float32)
    ang_row = (row[:, None] * inv[None, :]).astype(np.float32)
    ang_col = (col[:, None] * inv[None, :]).astype(np.float32)
    cos_h = np.concatenate([np.cos(ang_row), np.cos(ang_row), np.cos(ang_col), np.cos(ang_col)], axis=-1)
    sin_h = np.concatenate([-np.sin(ang_row), np.sin(ang_row), -np.sin(ang_col), np.sin(ang_col)], axis=-1)
    cos_q = np.tile(cos_h, (1, WIN_HEADS)).astype(np.float32)
    sin_q = np.tile(sin_h, (1, WIN_HEADS)).astype(np.float32)
    return cos_q, sin_q


def _rope(x, cos, sin_signed):
    width = x.shape[-1]
    q = HEAD_DIM // 4
    lane = lax.broadcasted_iota(jnp.int32, x.shape, 1)
    partner = jnp.where((lane & (2 * q - 1)) < q, pltpu.roll(x, width - q, 1), pltpu.roll(x, q, 1))
    return x * cos + partner * sin_signed


def _win_attn_kernel(sink_ref, q_ref, kp_ref, kc_ref, kn_ref, vp_ref, vc_ref, vn_ref, ck_ref, cv_ref,
                     cosq_ref, sinq_ref, cosp_ref, sinp_ref, cosc_ref, sinc_ref, cosn_ref, sinn_ref,
                     o_prev_ref, o_ref, *, layer):
    del o_prev_ref
    n = pl.program_id(1)
    nb = DEC_SEQ // WIN_BLOCK
    low = _lane_is_low(WIN_BLOCK)

    q = _rope(q_ref[...], cosq_ref[...], sinq_ref[...]) * ATT_SCALE
    k_loc = jnp.concatenate([_rope(kp_ref[...], cosp_ref[...], sinp_ref[...]),
                             _rope(kc_ref[...], cosc_ref[...], sinc_ref[...]),
                             _rope(kn_ref[...], cosn_ref[...], sinn_ref[...])], axis=0).astype(BF16)
    v_loc = jnp.concatenate([vp_ref[...], vc_ref[...], vn_ref[...]], axis=0).astype(BF16)
    k_ctx = ck_ref[...].astype(BF16)
    v_ctx = cv_ref[...].astype(BF16)

    qi = lax.broadcasted_iota(jnp.int32, (WIN_BLOCK, 3 * WIN_BLOCK), 0)
    kj = lax.broadcasted_iota(jnp.int32, (WIN_BLOCK, 3 * WIN_BLOCK), 1)
    rel = kj - WIN_BLOCK - qi
    valid = jnp.logical_and(rel >= -WINDOW, rel <= WINDOW)
    valid = jnp.logical_and(valid, kj >= jnp.where(n > 0, 0, WIN_BLOCK))
    valid = jnp.logical_and(valid, kj < jnp.where(n < nb - 1, 3 * WIN_BLOCK, 2 * WIN_BLOCK))

    valid = jnp.concatenate([valid] * WIN_GROUP, axis=0)
    band = lambda s: jnp.where(valid, s, NEG_INF)

    def segments_of(kt):
        cols = slice(kt * LANES, (kt + 1) * LANES)
        return [(k_loc[:, cols], v_loc[:, cols], band), (k_ctx[:, cols], v_ctx[:, cols], None)]

    o_ref[...] = _gqa_heads(lambda j: q[:, j * LANES:(j + 1) * LANES], segments_of, sink_ref, layer, low, WIN_BLOCK)


def _win_attention(proj, cache_k, cache_v, win_sink, tables, o_ctx, layer):
    nb = DEC_SEQ // WIN_BLOCK
    row0 = N_CTX // WIN_BLOCK
    cos_q, sin_q, cos_k, sin_k = tables
    prev = lambda n: jnp.maximum(n - 1, 0)
    cur = lambda n: n
    nxt = lambda n: jnp.minimum(n + 1, nb - 1)
    kv_spec = lambda off, f: pl.BlockSpec((WIN_BLOCK, WIN_KV), lambda b, n: (row0 + b * nb + f(n), off // WIN_KV))
    tab_q = pl.BlockSpec((WIN_BLOCK, 768), lambda b, n: (n, 0))
    tab_k = lambda f: pl.BlockSpec((WIN_BLOCK, WIN_KV), lambda b, n: (f(n), 0))
    cache_spec = pl.BlockSpec((None, None, PAST_LEN, WIN_KV), lambda b, n: (b, layer, 0, 0))
    return pl.pallas_call(
        functools.partial(_win_attn_kernel, layer=layer),
        grid=(DEC_BATCH, nb),
        in_specs=[pl.BlockSpec(memory_space=pltpu.SMEM),
                  pl.BlockSpec((WIN_BLOCK, 768), lambda b, n: (row0 + b * nb + n, QW_OFF // 768)),
                  kv_spec(KW_OFF, prev), kv_spec(KW_OFF, cur), kv_spec(KW_OFF, nxt),
                  kv_spec(VW_OFF, prev), kv_spec(VW_OFF, cur), kv_spec(VW_OFF, nxt),
                  cache_spec, cache_spec,
                  tab_q, tab_q, tab_k(prev), tab_k(prev), tab_k(cur), tab_k(cur), tab_k(nxt), tab_k(nxt),
                  pl.BlockSpec(memory_space=pl.ANY)],
        out_specs=pl.BlockSpec((WIN_BLOCK, 768), lambda b, n: (row0 + b * nb + n, 0)),
        out_shape=jax.ShapeDtypeStruct((N_TOK, 768), BF16),
        input_output_aliases={18: 0},
        compiler_params=_cparams(("parallel", "parallel"), 32),
        name="win_attention",
    )(win_sink, proj, proj, proj, proj, proj, proj, proj, cache_k, cache_v,
      cos_q, sin_q, cos_k, sin_k, cos_k, sin_k, cos_k, sin_k, o_ctx)


def _na_window_start(r0):
    return min(max(r0 - NA_ROWS // 2, 0), GRID_H - NA_WIN_ROWS)


def _na_bias_kernel(rpb_ref, o_ref):
    head = pl.program_id(0)
    base = head * (RPB_H * RPB_W)
    qc = lax.broadcasted_iota(jnp.int32, (GRID_W, LANES), 0)
    lane = lax.broadcasted_iota(jnp.int32, (GRID_W, LANES), 1)
    kc = lane & (GRID_W - 1)
    low = lane < HALF
    dc_idx = jnp.clip(kc - qc + NA_COLS - 1, 0, RPB_W - 1)
    q_start = jnp.clip(qc - NA_COLS // 2, 0, GRID_W - NA_COLS)
    col_valid = jnp.logical_and(kc >= q_start, kc < q_start + NA_COLS)
    neg = jnp.full((GRID_W, LANES), NEG_INF, F32)

    pair = {}
    for d in range(-NA_ROWS, NA_ROWS):
        acc = jnp.zeros((GRID_W, LANES), F32)
        for dc in range(RPB_W):
            s_lo = rpb_ref[base + (d + NA_ROWS - 1) * RPB_W + dc] if abs(d) < NA_ROWS else 0.0
            s_hi = rpb_ref[base + (d + NA_ROWS) * RPB_W + dc] if abs(d + 1) < NA_ROWS else 0.0
            acc = jnp.where(dc_idx == dc, jnp.where(low, s_lo, s_hi), acc)
        pair[d] = jnp.where(col_valid, acc, NEG_INF)

    for qb in range(NA_QBLOCKS):
        r0 = NA_QROWS * qb
        ws = _na_window_start(r0)
        for qi in range(NA_QROWS):
            qr = r0 + qi
            rs = min(max(qr - NA_ROWS // 2, 0), GRID_H - NA_ROWS)
            for p in range(NA_WIN_ROWS // 2):
                kr = ws + 2 * p
                ok_lo = rs <= kr < rs + NA_ROWS
                ok_hi = rs <= kr + 1 < rs + NA_ROWS
                d = kr - qr
                if not (ok_lo or ok_hi):
                    tile = neg
                else:
                    tile = pair[d]
                    if not ok_lo:
                        tile = jnp.where(low, NEG_INF, tile)
                    if not ok_hi:
                        tile = jnp.where(low, tile, NEG_INF)
                o_ref[qb, qi * GRID_W:(qi + 1) * GRID_W, p * LANES:(p + 1) * LANES] = tile


def _na_bias(rpb_l):
    return pl.pallas_call(
        _na_bias_kernel,
        grid=(NA_HEADS,),
        in_specs=[pl.BlockSpec(memory_space=pltpu.SMEM)],
        out_specs=pl.BlockSpec((None, NA_QBLOCKS, NA_QROWS * GRID_W, NA_WIN_KEYS), lambda h: (h, 0, 0, 0)),
        out_shape=jax.ShapeDtypeStruct((NA_HEADS, NA_QBLOCKS, NA_QROWS * GRID_W, NA_WIN_KEYS), F32),
        compiler_params=_cparams(("parallel",), 32),
        name="na_bias",
    )(rpb_l.reshape(NA_HEADS * RPB_H * RPB_W))


def _na_attn_kernel(q_ref, k_ref, v_ref, ck_ref, cv_ref, bias_ref, o_prev_ref, o_ref):
    del o_prev_ref
    nq = NA_QROWS * GRID_W
    low = _lane_is_low(nq)
    k_all = k_ref[...].astype(BF16)
    v_all = v_ref[...].astype(BF16)
    k_ctx = ck_ref[...].astype(BF16)
    v_ctx = cv_ref[...].astype(BF16)
    blocks = []
    for qb in range(NA_QBLOCKS):
        start = _na_window_start(NA_QROWS * qb) * GRID_W
        k_loc = k_all[start:start + NA_WIN_KEYS]
        v_loc = v_all[start:start + NA_WIN_KEYS]
        bias = jnp.concatenate([bias_ref[0, qb], bias_ref[1, qb]], axis=0)
        q2 = q_ref[qb * nq:(qb + 1) * nq, :] * ATT_SCALE
        blocks.append(_pair_heads(q2, [(k_loc, v_loc, lambda s, bias=bias: s + bias), (k_ctx, v_ctx, None)], low, nq))
    o_ref[...] = jnp.concatenate(blocks, axis=0)


def _na_attention(proj, cache_k, cache_v, bias, o_ctx, layer):
    nq = NA_QROWS * GRID_W
    seq_blk0 = N_CTX // DEC_SEQ
    seq_spec = lambda off: pl.BlockSpec((DEC_SEQ, LANES), lambda hp, b: (seq_blk0 + b, off // LANES + hp))
    cache_spec = pl.BlockSpec((None, None, PAST_LEN, LANES), lambda hp, b: (b, layer, 0, hp))
    return pl.pallas_call(
        _na_attn_kernel,
        grid=(NA_HEADS // 2, DEC_BATCH),
        in_specs=[seq_spec(QN_OFF), seq_spec(KN_OFF), seq_spec(VN_OFF), cache_spec, cache_spec,
                  pl.BlockSpec((2, NA_QBLOCKS, nq, NA_WIN_KEYS), lambda hp, b: (hp, 0, 0, 0)),
                  pl.BlockSpec(memory_space=pl.ANY)],
        out_specs=pl.BlockSpec((DEC_SEQ, LANES), lambda hp, b: (seq_blk0 + b, hp)),
        out_shape=jax.ShapeDtypeStruct((N_TOK, 768), BF16),
        input_output_aliases={6: 0},
        compiler_params=_cparams(("parallel", "parallel"), 40),
        name="na_attention",
    )(proj, proj, proj, cache_k, cache_v, bias, o_ctx)


def _ssm_prep_kernel(lr_ref, li_ref, ldt_ref, br_ref, bi_ref, lbr_ref, lbi_ref, bbr_ref, bbi_ref):
    lr, li = lr_ref[...], li_ref[...]
    dt = jnp.exp(ldt_ref[...])
    mag = jnp.exp(lr * dt)
    ang = li * dt
    lbr = mag * jnp.cos(ang)
    lbi = mag * jnp.sin(ang)
    nr = lbr - 1.0
    den = lr * lr + li * li
    cr = (nr * lr + lbi * li) / den
    ci = (lbi * lr - nr * li) / den
    br, bi = br_ref[...], bi_ref[...]
    lbr_ref[...] = lbr
    lbi_ref[...] = lbi
    bbr_ref[...] = cr * br - ci * bi
    bbi_ref[...] = cr * bi + ci * br


def _ssm_prep(lam_re, lam_im, log_dt, b_re, b_im):
    shape5 = (DEPTH, 2, SSM_GROUPS, SSM_GROUP, SSM_STATE)
    rows = DEPTH * 2 * SSM_GROUPS * SSM_GROUP
    bc = lambda t: jnp.broadcast_to(t, shape5).reshape(rows, SSM_STATE)
    args = (bc(lam_re[:, :, :, None, :]), bc(lam_im[:, :, :, None, :]), bc(log_dt[:, :, :, None, None]),
            jnp.swapaxes(b_re, -1, -2).reshape(rows, SSM_STATE), jnp.swapaxes(b_im, -1, -2).reshape(rows, SSM_STATE))
    outs = pl.pallas_call(
        _ssm_prep_kernel,
        out_shape=[jax.ShapeDtypeStruct((rows, SSM_STATE), F32)] * 4,
        compiler_params=_cparams(None, 32),
        name="ssm_prep",
    )(*args)
    lbr, lbi, bbr, bbi = [o.reshape(shape5) for o in outs]
    return lbr[:, :, :, 0], lbi[:, :, :, 0], bbr, bbi


def _scan_permutation():
    p = np.zeros((SCAN_ROWS, SCAN_ROWS), np.float32)
    for t in range(SCAN_T):
        for s in range(SUBLANES):
            src_t = t if s < SUBLANES // 2 else SCAN_T - 1 - t
            p[t * SUBLANES + s, s * SCAN_T + src_t] = 1.0
    return p


def _ssm_scan_kernel(*refs, aliased):
    uf_ref, ub_ref, h0_ref, lam_ref, perm_ref, permt_ref, b_ref, c_ref = refs[:8]
    yf_ref, yb_ref, fin_ref, st_ref, bu_ref, y_ref = refs[8 + (2 if aliased else 0):]
    half_rows = SCAN_ROWS // 2

    @pl.when(pl.program_id(1) == 0)
    def _():
        st_ref[...] = h0_ref[...]

    u_bm = jnp.concatenate([uf_ref[...].reshape(half_rows, SSM_WIDTH),
                            ub_ref[...].reshape(half_rows, SSM_WIDTH)], axis=0).astype(BF16)
    u_tm = jnp.dot(perm_ref[...], u_bm, preferred_element_type=F32).astype(BF16)

    fwd = (lax.broadcasted_iota(jnp.int32, (SCAN_ROWS, 1), 0) & (SUBLANES - 1)) < SUBLANES // 2
    zero = jnp.zeros((SCAN_ROWS, LANES), BF16)
    for kb in range(SSM_KB):
        ub = u_tm[:, kb * LANES:(kb + 1) * LANES]
        lhs = jnp.concatenate([jnp.where(fwd, ub, zero), jnp.where(fwd, zero, ub)], axis=1)
        bu_ref[...] = jnp.dot(lhs, b_ref[kb], preferred_element_type=F32)
        ar = lam_ref[kb, 0]
        ai = lam_ref[kb, 1]

        def step(t, carry):
            xr, xi = carry
            r = pl.multiple_of(t * SUBLANES, SUBLANES)
            nr = ar * xr - ai * xi + bu_ref[pl.ds(r, SUBLANES), 0:STATE_COLS]
            ni = ar * xi + ai * xr + bu_ref[pl.ds(r, SUBLANES), STATE_COLS:2 * STATE_COLS]
            bu_ref[pl.ds(r, SUBLANES), 0:STATE_COLS] = nr
            bu_ref[pl.ds(r, SUBLANES), STATE_COLS:2 * STATE_COLS] = ni
            return nr, ni

        xr, xi = lax.fori_loop(0, SCAN_T, step, (st_ref[kb, 0], st_ref[kb, 1]))
        st_ref[kb, 0] = xr
        st_ref[kb, 1] = xi
        xs = bu_ref[...].astype(BF16)
        yy = jnp.dot(xs, c_ref[kb], preferred_element_type=F32)
        y_ref[:, kb * LANES:(kb + 1) * LANES] = jnp.where(fwd, yy[:, :LANES], yy[:, LANES:])
    fin_ref[...] = st_ref[...]

    y = y_ref[...]
    hi = y.astype(BF16)
    rest = y - hi.astype(F32)
    mid = rest.astype(BF16)
    low = (rest - mid.astype(F32)).astype(BF16)
    permt = permt_ref[...]
    y_bm = (jnp.dot(permt, hi, preferred_element_type=F32) + jnp.dot(permt, mid, preferred_element_type=F32)
            + jnp.dot(permt, low, preferred_element_type=F32))
    yf_ref[...] = y_bm[:half_rows].reshape(SUBLANES // 2, SCAN_T, SSM_WIDTH)
    yb_ref[...] = y_bm[half_rows:].reshape(SUBLANES // 2, SCAN_T, SSM_WIDTH)


def _ssm_scan(proj, seq_len, first_seq, n_seq, h0, lam, bmat, cmat, y_prev=None):
    n_all = N_TOK // seq_len
    nc = seq_len // SCAN_T
    blk0 = first_seq // 4
    proj3 = proj.reshape(n_all, seq_len, N_PROJ)
    state_shape = (SSM_KB, 2, SUBLANES, STATE_COLS)
    whole = lambda shape: pl.BlockSpec(shape, lambda b, c: (0,) * len(shape))
    state_spec = pl.BlockSpec((None,) + state_shape, lambda b, c: (b, 0, 0, 0, 0))
    u_spec = lambda chunk: pl.BlockSpec((4, SCAN_T, SSM_WIDTH), lambda b, c: (blk0 + b, chunk(c), U_OFF // SSM_WIDTH))
    y_spec = lambda chunk: pl.BlockSpec((4, SCAN_T, SSM_WIDTH), lambda b, c: (blk0 + b, chunk(c), 0))
    ascending = lambda c: c
    descending = lambda c: nc - 1 - c
    perm = _scan_permutation()
    aliased = y_prev is not None
    extra_specs = [pl.BlockSpec(memory_space=pl.ANY)] * 2 if aliased else []
    extra_args = [t.reshape(n_all, seq_len, SSM_WIDTH) for t in y_prev] if aliased else []
    yf, yb, fin = pl.pallas_call(
        functools.partial(_ssm_scan_kernel, aliased=aliased),
        grid=(n_seq // 4, nc),
        in_specs=[u_spec(ascending), u_spec(descending), state_spec, whole(state_shape),
                  whole((SCAN_ROWS, SCAN_ROWS)), whole((SCAN_ROWS, SCAN_ROWS)),
                  whole((SSM_KB, 2 * LANES, 2 * STATE_COLS)), whole((SSM_KB, 2 * STATE_COLS, 2 * LANES))] + extra_specs,
        out_specs=[y_spec(ascending), y_spec(descending), state_spec],
        out_shape=[jax.ShapeDtypeStruct((n_all, seq_len, SSM_WIDTH), F32)] * 2
                  + [jax.ShapeDtypeStruct((n_seq // 4,) + state_shape, F32)],
        scratch_shapes=[pltpu.VMEM(state_shape, F32), pltpu.VMEM((SCAN_ROWS, 2 * STATE_COLS), F32),
                        pltpu.VMEM((SCAN_ROWS, SSM_WIDTH), F32)],
        input_output_aliases={8: 0, 9: 1} if aliased else {},
        compiler_params=_cparams(("parallel", "arbitrary"), 40),
        name="ssm_scan",
    )(proj3, proj3, h0, lam, jnp.asarray(perm, BF16), jnp.asarray(perm.T, BF16), bmat, cmat, *extra_args)
    return yf.reshape(N_TOK, SSM_WIDTH), yb.reshape(N_TOK, SSM_WIDTH), fin


def _ssm_scan_params(lbr, lbi, bbr, bbi, c_re, c_im):
    eye = jnp.eye(GROUPS_PER_KB, dtype=F32)

    def lam_tile(t):
        t = t.reshape(DEPTH, 2, SSM_KB, STATE_COLS).transpose(0, 2, 1, 3)
        return jnp.repeat(t, SUBLANES // 2, axis=2)

    def b_blocks(t):
        t = t.reshape(DEPTH, 2, SSM_KB, GROUPS_PER_KB, SSM_GROUP, SSM_STATE)
        return jnp.einsum('ldkgnp,gh->ldkgnhp', t, eye).reshape(DEPTH, 2, SSM_KB, LANES, STATE_COLS)

    def c_blocks(t):
        t = t.reshape(DEPTH, 2, SSM_KB, GROUPS_PER_KB, SSM_GROUP, SSM_STATE)
        return jnp.einsum('ldkgnp,gh->ldkgphn', t, eye).reshape(DEPTH, 2, SSM_KB, STATE_COLS, LANES)

    lam = jnp.stack([lam_tile(lbr), lam_tile(lbi)], axis=2)
    b = jnp.concatenate([b_blocks(bbr), b_blocks(bbi)], axis=-1)
    bmat = jnp.concatenate([b[:, 0], b[:, 1]], axis=-2).astype(BF16)
    c = jnp.concatenate([c_blocks(c_re), -c_blocks(c_im)], axis=-2)
    cmat = jnp.concatenate([c[:, 0], c[:, 1]], axis=-1).astype(BF16)
    return lam, bmat, cmat


def _ssm_glu_kernel(yf_ref, yb_ref, u_ref, d_ref, w_ref, b_ref, o_ref):
    y = yf_ref[...] + yb_ref[...] + d_ref[...] * u_ref[...]
    y = jax.nn.gelu(y)
    z = jnp.dot(y.astype(BF16), w_ref[...], preferred_element_type=F32) + b_ref[...]
    o_ref[...] = (y * jax.nn.sigmoid(z)).astype(BF16)


def _ssm_glu(yf, yb, proj, d_skip, w_glu, b_glu, layer):
    tm = 512
    row = pl.BlockSpec((tm, SSM_WIDTH), lambda i: (i, 0))
    vec = pl.BlockSpec((None, 1, SSM_WIDTH), lambda i: (layer, 0, 0))
    return pl.pallas_call(
        _ssm_glu_kernel,
        grid=(N_TOK // tm,),
        in_specs=[row, row, pl.BlockSpec((tm, SSM_WIDTH), lambda i: (i, U_OFF // SSM_WIDTH)), vec,
                  pl.BlockSpec((None, SSM_WIDTH, SSM_WIDTH), lambda i: (layer, 0, 0)), vec],
        out_specs=row,
        out_shape=jax.ShapeDtypeStruct((N_TOK, SSM_WIDTH), BF16),
        compiler_params=_cparams(("parallel",), 32),
        name="ssm_glu",
    )(yf, yb, proj, d_skip.reshape(DEPTH, 1, SSM_WIDTH), w_glu, b_glu.reshape(DEPTH, 1, SSM_WIDTH))


def _ssm_mixer(proj, lam, bmat, cmat, h0_lat, d_skip, w_glu, b_glu, layer):
    h0_ctx = jnp.zeros((BATCH // 4, SSM_KB, 2, SUBLANES, STATE_COLS), F32)
    yf, yb, fin = _ssm_scan(proj, SEQ, 0, BATCH, h0_ctx, lam, bmat, cmat)
    yf, yb, _ = _ssm_scan(proj, DEC_SEQ, N_CTX // DEC_SEQ, DEC_BATCH, h0_lat, lam, bmat, cmat, y_prev=(yf, yb))
    return _ssm_glu(yf, yb, proj, d_skip, w_glu, b_glu, layer), fin


def _state_to_tiles(s_re, s_im):
    def one(t):
        t = t.transpose(1, 0, 2, 3).reshape(SUBLANES, SSM_KB, STATE_COLS)
        return t.transpose(1, 0, 2)
    return jnp.stack([one(s_re), one(s_im)], axis=1)[None]


def _tiles_to_state(fin, part):
    t = fin[:, :, part].reshape(BATCH // 4, SSM_KB, 2, 4, GROUPS_PER_KB, SSM_STATE)
    return t.transpose(0, 3, 2, 1, 4, 5).reshape(BATCH, 2, SSM_GROUPS, SSM_STATE)


def kernel(x_prompt, x_sample, cache_win_k, cache_win_v, cache_na_k, cache_na_v, state_ssm_re, state_ssm_im,
           c, c_ctx, norm1_g, norm2_g, w_mod, b_mod, w_in, ssm_lam_re, ssm_lam_im, ssm_log_dt,
           ssm_b_re, ssm_b_im, ssm_c_re, ssm_c_im, ssm_d, w_glu, b_glu, win_sink, na_rpb,
           w_branch, w_out, w_up, conv_w, conv_b, w_down, final_g):
    x = jnp.concatenate([x_prompt.reshape(N_CTX, D_MODEL), x_sample.reshape(N_LAT, D_MODEL)], axis=0)

    w_in_b, w_glu_b, w_branch_b = w_in.astype(BF16), w_glu.astype(BF16), w_branch.astype(BF16)
    w_out_b, w_up_b, w_down_b = w_out.astype(BF16), w_up.astype(BF16), w_down.astype(BF16)

    cond8 = jnp.concatenate([c_ctx[None], c, jnp.zeros((SUBLANES - 1 - DEC_BATCH, D_MODEL), F32)], axis=0)
    mods = _adaln(cond8, w_mod, b_mod)
    mods = mods.reshape(DEPTH, SUBLANES, 6, D_MODEL).transpose(0, 2, 1, 3)[:, :, :, None, :]

    lbr, lbi, bbr, bbi = _ssm_prep(ssm_lam_re, ssm_lam_im, ssm_log_dt, ssm_b_re, ssm_b_im)
    lam, bmat, cmat = _ssm_scan_params(lbr, lbi, bbr, bbi, ssm_c_re, ssm_c_im)

    cos_q, sin_q = _rope_tables()
    tables = (jnp.asarray(cos_q), jnp.asarray(sin_q), jnp.asarray(cos_q[:, :WIN_KV]), jnp.asarray(sin_q[:, :WIN_KV]))
    cwk = cache_win_k.reshape(DEC_BATCH, DEPTH, PAST_LEN, WIN_KV)
    cwv = cache_win_v.reshape(DEC_BATCH, DEPTH, PAST_LEN, WIN_KV)
    cnk = cache_na_k.reshape(DEC_BATCH, DEPTH, PAST_LEN, NA_HEADS * HEAD_DIM)
    cnv = cache_na_v.reshape(DEC_BATCH, DEPTH, PAST_LEN, NA_HEADS * HEAD_DIM)

    new_wk, new_wv, new_nk, new_nv, new_sre, new_sim = [], [], [], [], [], []
    for l in range(DEPTH):
        gates, proj = _in_proj(x, norm1_g, mods, w_in_b, l)

        h0_lat = _state_to_tiles(state_ssm_re[:, l], state_ssm_im[:, l])
        o_ssm, fin = _ssm_mixer(proj, lam[l], bmat[l], cmat[l], h0_lat, ssm_d, w_glu_b, b_glu, l)

        o_win, o_na = _ctx_attention(proj, win_sink, l)
        o_win = _win_attention(proj, cwk, cwv, win_sink, tables, o_win, l)
        o_na = _na_attention(proj, cnk, cnv, _na_bias(na_rpb[l]), o_na, l)

        merged = _branch_merge(o_ssm, o_win, o_na, w_branch_b, gates, l)
        x, h2 = _out_proj(merged, w_out_b, x, mods, norm2_g, l)
        act = _ffn_up(h2, w_up_b, conv_w, conv_b, l)
        x = _matmul_residual(act, w_down_b, x, mods, l, 5, tm=1024, tn=512)

        ctx = proj[:N_CTX]
        new_wk.append(ctx[:, KW_OFF:KW_OFF + WIN_KV].reshape(BATCH, SEQ, WIN_KV_HEADS, HEAD_DIM))
        new_wv.append(ctx[:, VW_OFF:VW_OFF + WIN_KV].reshape(BATCH, SEQ, WIN_KV_HEADS, HEAD_DIM))
        new_nk.append(ctx[:, KN_OFF:KN_OFF + 768].reshape(BATCH, SEQ, NA_HEADS, HEAD_DIM))
        new_nv.append(ctx[:, VN_OFF:VN_OFF + 768].reshape(BATCH, SEQ, NA_HEADS, HEAD_DIM))
        new_sre.append(_tiles_to_state(fin, 0))
        new_sim.append(_tiles_to_state(fin, 1))

    y_ctx = _final_norm(x, final_g, 0, N_CTX)
    y_lat = _final_norm(x, final_g, N_CTX, N_LAT)
    return (y_ctx.reshape(BATCH, SEQ, D_MODEL), y_lat.reshape(DEC_BATCH, DEC_SEQ, D_MODEL),
            jnp.stack(new_wk, axis=1), jnp.stack(new_wv, axis=1), jnp.stack(new_nk, axis=1),
            jnp.stack(new_nv, axis=1), jnp.stack(new_sre, axis=1), jnp.stack(new_sim, axis=1))
```

```python
import functools

import numpy as np
import jax
import jax.numpy as jnp
from jax import lax
from jax.experimental import pallas as pl
from jax.experimental.pallas import tpu as pltpu

F32 = jnp.float32
BF16 = jnp.bfloat16

D_MODEL = 2048
BATCH = 16
SEQ = 256
DEPTH = 4
DEC_BATCH = 4
DEC_SEQ = 1024
PAST_LEN = 512
GRID_W = 64
GRID_H = DEC_SEQ // GRID_W
HEAD_DIM = 64
SSM_WIDTH = 768
SSM_GROUP = 16
SSM_GROUPS = SSM_WIDTH // SSM_GROUP
SSM_STATE = 64
WIN_HEADS = 12
WIN_KV_HEADS = 4
WIN_GROUP = WIN_HEADS // WIN_KV_HEADS
WINDOW = 128
WIN_BLOCK = 128
NA_HEADS = 12
NA_ROWS = 8
NA_COLS = 16
BRANCH_W = 768
D_FF = 5632
CONV_W = 3
ROPE_BASE = 10000.0
EPS = 1e-6
NEG_INF = -1e30
ATT_SCALE = HEAD_DIM ** -0.5
WIN_KV = WIN_KV_HEADS * HEAD_DIM

N_CTX = BATCH * SEQ
N_LAT = DEC_BATCH * DEC_SEQ
N_TOK = N_CTX + N_LAT

LANES = 128
SUBLANES = 8
HALF = LANES // 2
MIB = 1024 * 1024

N_GATE = 3 * D_MODEL
U_OFF = 0
QW_OFF = U_OFF + 768
QN_OFF = QW_OFF + 768
KN_OFF = QN_OFF + 768
VN_OFF = KN_OFF + 768
KW_OFF = VN_OFF + 768
VW_OFF = KW_OFF + WIN_KV
IN_TN = 1536
IN_SUB = 256
IN_SUBS = IN_TN // IN_SUB
N_PROJ = -(-(VW_OFF + WIN_KV) // IN_TN) * IN_TN

SSM_KB = SSM_WIDTH // LANES
GROUPS_PER_KB = LANES // SSM_GROUP
STATE_COLS = GROUPS_PER_KB * SSM_STATE
SCAN_T = 64
SCAN_ROWS = SCAN_T * SUBLANES

NA_QROWS = 2
NA_QBLOCKS = GRID_H // NA_QROWS
NA_WIN_ROWS = 10
NA_WIN_KEYS = NA_WIN_ROWS * GRID_W
RPB_H = 2 * NA_ROWS - 1
RPB_W = 2 * NA_COLS - 1


def _cparams(dims, vmem_mib):
    return pltpu.CompilerParams(dimension_semantics=dims, vmem_limit_bytes=int(vmem_mib * MIB))


def _mod_row(i, tm):
    tiles_ctx = N_CTX // tm
    per_batch = DEC_SEQ // tm
    return jnp.where(i < tiles_ctx, 0, 1 + (i - tiles_ctx) // per_batch)


def _mod_spec(layer, which, tm, tn, col_of):
    return pl.BlockSpec((None, None, None, 1, tn),
                        lambda i, j: (layer, which, _mod_row(i, tm), 0, col_of(i, j)))


def _adaln_kernel(c_ref, w_ref, b_ref, o_ref):
    s = jax.nn.silu(c_ref[...]).astype(BF16)
    o_ref[...] = jnp.dot(s, w_ref[...].astype(BF16), preferred_element_type=F32) + b_ref[...]


def _adaln(cond8, w_mod, b_mod):
    tn = 1024
    n = 6 * D_MODEL
    return pl.pallas_call(
        _adaln_kernel,
        grid=(DEPTH, n // tn),
        in_specs=[pl.BlockSpec((SUBLANES, D_MODEL), lambda l, j: (0, 0)),
                  pl.BlockSpec((None, D_MODEL, tn), lambda l, j: (l, 0, j)),
                  pl.BlockSpec((None, 1, tn), lambda l, j: (l, 0, j))],
        out_specs=pl.BlockSpec((None, SUBLANES, tn), lambda l, j: (l, 0, j)),
        out_shape=jax.ShapeDtypeStruct((DEPTH, SUBLANES, n), F32),
        compiler_params=_cparams(("parallel", "parallel"), 40),
        name="adaln",
    )(cond8, w_mod, b_mod.reshape(DEPTH, 1, n))


def _norm_mod(x, g, sh, sc):
    y = x * lax.rsqrt(jnp.mean(x * x, axis=-1, keepdims=True) + EPS)
    y = y * g
    return y * (1.0 + sc) + sh


def _in_proj_source_block(g):
    n_gate, n_uq, n_na, n_kvw = (N_GATE // IN_SUB, 2 * 768 // IN_SUB, 3 * 768 // IN_SUB, 2 * WIN_KV // IN_SUB)
    src_gate, src_na, src_kvw = 4352 // IN_SUB, 2048 // IN_SUB, 1536 // IN_SUB
    b1, b2, b3 = n_gate + n_uq, n_gate + n_uq + n_na, n_gate + n_uq + n_na + n_kvw
    return jnp.where(g < n_gate, src_gate + g,
                     jnp.where(g < b1, g - n_gate,
                               jnp.where(g < b2, src_na + g - b1,
                                         jnp.where(g < b3, src_kvw + g - b2, 0))))


def _in_proj_kernel(*refs):
    x_ref, g_ref, sh_ref, sc_ref = refs[:4]
    w_refs = refs[4:4 + IN_SUBS]
    gate_ref, proj_ref, h_ref = refs[4 + IN_SUBS:]
    j = pl.program_id(1)

    @pl.when(j == 0)
    def _():
        h_ref[...] = _norm_mod(x_ref[...], g_ref[...], sh_ref[...], sc_ref[...]).astype(BF16)

    def sub_blocks():
        for r, w_ref in enumerate(w_refs):
            yield slice(r * IN_SUB, (r + 1) * IN_SUB), jnp.dot(h_ref[...], w_ref[...], preferred_element_type=F32)

    @pl.when(j < N_GATE // IN_TN)
    def _():
        for cols, acc in sub_blocks():
            gate_ref[:, cols] = (0.5 * jnp.tanh(0.5 * acc) + 0.5).astype(BF16)

    @pl.when(j >= N_GATE // IN_TN)
    def _():
        for cols, acc in sub_blocks():
            proj_ref[:, cols] = acc


def _in_proj(x, norm_g, mods, w_in_b, layer):
    tm = 1024
    n_gate_tiles = N_GATE // IN_TN
    zero = lambda i, j: 0
    w_spec = lambda r: pl.BlockSpec((None, D_MODEL, IN_SUB),
                                    lambda i, j: (layer, 0, _in_proj_source_block(IN_SUBS * j + r)))
    return pl.pallas_call(
        _in_proj_kernel,
        grid=(N_TOK // tm, (N_GATE + N_PROJ) // IN_TN),
        in_specs=[pl.BlockSpec((tm, D_MODEL), lambda i, j: (i, 0)),
                  pl.BlockSpec((None, 1, D_MODEL), lambda i, j: (layer, 0, 0)),
                  _mod_spec(layer, 0, tm, D_MODEL, zero),
                  _mod_spec(layer, 1, tm, D_MODEL, zero)] + [w_spec(r) for r in range(IN_SUBS)],
        out_specs=[pl.BlockSpec((tm, IN_TN), lambda i, j: (i, jnp.minimum(j, n_gate_tiles - 1))),
                   pl.BlockSpec((tm, IN_TN), lambda i, j: (i, jnp.maximum(j - n_gate_tiles, 0)))],
        out_shape=[jax.ShapeDtypeStruct((N_TOK, N_GATE), BF16), jax.ShapeDtypeStruct((N_TOK, N_PROJ), F32)],
        scratch_shapes=[pltpu.VMEM((tm, D_MODEL), BF16)],
        compiler_params=_cparams(("parallel", "arbitrary"), 60),
        name="in_proj",
    )(x, norm_g.reshape(DEPTH, 1, D_MODEL), mods, mods, *([w_in_b] * IN_SUBS))


def _matmul_residual_kernel(a_ref, w_ref, x_ref, g_ref, o_ref):
    acc = jnp.dot(a_ref[...], w_ref[...], preferred_element_type=F32)
    o_ref[...] = x_ref[...] + g_ref[...] * acc


def _matmul_residual(a, w, x, mods, layer, which, tm, tn):
    k = a.shape[-1]
    return pl.pallas_call(
        _matmul_residual_kernel,
        grid=(N_TOK // tm, D_MODEL // tn),
        in_specs=[pl.BlockSpec((tm, k), lambda i, j: (i, 0)),
                  pl.BlockSpec((None, k, tn), lambda i, j: (layer, 0, j)),
                  pl.BlockSpec((tm, tn), lambda i, j: (i, j)),
                  _mod_spec(layer, which, tm, tn, lambda i, j: j)],
        out_specs=pl.BlockSpec((tm, tn), lambda i, j: (i, j)),
        out_shape=jax.ShapeDtypeStruct((N_TOK, D_MODEL), F32),
        compiler_params=_cparams(("parallel", "arbitrary"), 52),
        name="matmul_residual",
    )(a, w, x, mods)


def _out_proj_kernel(m_ref, w_ref, x_ref, g1_ref, ng_ref, sh_ref, sc_ref, xo_ref, h_ref):
    x_new = x_ref[...] + g1_ref[...] * jnp.dot(m_ref[...], w_ref[...], preferred_element_type=F32)
    xo_ref[...] = x_new
    h_ref[...] = _norm_mod(x_new, ng_ref[...], sh_ref[...], sc_ref[...]).astype(BF16)


def _out_proj(merged, w_out, x, mods, norm_g, layer):
    tm = 512
    row = pl.BlockSpec((tm, D_MODEL), lambda i: (i, 0))
    mod = lambda which: pl.BlockSpec((None, None, None, 1, D_MODEL), lambda i: (layer, which, _mod_row(i, tm), 0, 0))
    return pl.pallas_call(
        _out_proj_kernel,
        grid=(N_TOK // tm,),
        in_specs=[row, pl.BlockSpec((None, D_MODEL, D_MODEL), lambda i: (layer, 0, 0)), row,
                  mod(2), pl.BlockSpec((None, 1, D_MODEL), lambda i: (layer, 0, 0)), mod(3), mod(4)],
        out_specs=[row, row],
        out_shape=[jax.ShapeDtypeStruct((N_TOK, D_MODEL), F32), jax.ShapeDtypeStruct((N_TOK, D_MODEL), BF16)],
        compiler_params=_cparams(("parallel",), 52),
        name="out_proj",
    )(merged, w_out, x, mods, norm_g.reshape(DEPTH, 1, D_MODEL), mods, mods)


def _branch_merge_kernel(o0_ref, o1_ref, o2_ref, w0_ref, w1_ref, w2_ref, g0_ref, g1_ref, g2_ref, m_ref):
    m = g0_ref[...].astype(F32) * jnp.dot(o0_ref[...], w0_ref[...], preferred_element_type=F32)
    m = m + g1_ref[...].astype(F32) * jnp.dot(o1_ref[...], w1_ref[...], preferred_element_type=F32)
    m = m + g2_ref[...].astype(F32) * jnp.dot(o2_ref[...], w2_ref[...], preferred_element_type=F32)
    m_ref[...] = m.astype(BF16)


def _branch_merge(o_ssm, o_win, o_na, w_branch, gates, layer):
    tm, tn = 1024, 512
    nj = D_MODEL // tn
    o_spec = pl.BlockSpec((tm, BRANCH_W), lambda i, j: (i, 0))
    w_spec = lambda k: pl.BlockSpec((None, None, BRANCH_W, tn), lambda i, j: (layer, k, 0, j))
    g_spec = lambda k: pl.BlockSpec((tm, tn), lambda i, j: (i, k * nj + j))
    return pl.pallas_call(
        _branch_merge_kernel,
        grid=(N_TOK // tm, nj),
        in_specs=[o_spec, o_spec, o_spec, w_spec(0), w_spec(1), w_spec(2), g_spec(0), g_spec(1), g_spec(2)],
        out_specs=pl.BlockSpec((tm, tn), lambda i, j: (i, j)),
        out_shape=jax.ShapeDtypeStruct((N_TOK, D_MODEL), BF16),
        compiler_params=_cparams(("parallel", "arbitrary"), 48),
        name="branch_merge",
    )(o_ssm, o_win, o_na, w_branch, w_branch, w_branch, gates, gates, gates)


FFN_TM = 1024
FFN_TF = 512


def _ffn_up_kernel(h_ref, wa_ref, wb_ref, cwa_ref, cwb_ref, cba_ref, cbb_ref, o_ref):
    i = pl.program_id(0)
    seq = jnp.where(i < N_CTX // FFN_TM, SEQ, DEC_SEQ)
    pos = lax.broadcasted_iota(jnp.int32, (FFN_TM, 1), 0) & (seq - 1)
    first = pos == 0
    last = pos == seq - 1

    def conv(u, cw_ref, cb_ref):
        prev = jnp.where(first, 0.0, pltpu.roll(u, 1, 0))
        nxt = jnp.where(last, 0.0, pltpu.roll(u, FFN_TM - 1, 0))
        return prev * cw_ref[0:1, :] + u * cw_ref[1:2, :] + nxt * cw_ref[2:3, :] + cb_ref[...]

    h = h_ref[...]
    a = conv(jnp.dot(h, wa_ref[...].astype(BF16), preferred_element_type=F32), cwa_ref, cba_ref)
    b = conv(jnp.dot(h, wb_ref[...].astype(BF16), preferred_element_type=F32), cwb_ref, cbb_ref)
    o_ref[...] = (jax.nn.silu(a) * b).astype(BF16)


def _ffn_up(h2, w_up, conv_w, conv_b, layer):
    tm, tf = FFN_TM, FFN_TF
    nj = D_FF // tf
    conv_b3 = conv_b.reshape(DEPTH, 1, 2 * D_FF)
    return pl.pallas_call(
        _ffn_up_kernel,
        grid=(N_TOK // tm, nj),
        in_specs=[pl.BlockSpec((tm, D_MODEL), lambda i, j: (i, 0)),
                  pl.BlockSpec((None, D_MODEL, tf), lambda i, j: (layer, 0, j)),
                  pl.BlockSpec((None, D_MODEL, tf), lambda i, j: (layer, 0, nj + j)),
                  pl.BlockSpec((None, CONV_W, tf), lambda i, j: (layer, 0, j)),
                  pl.BlockSpec((None, CONV_W, tf), lambda i, j: (layer, 0, nj + j)),
                  pl.BlockSpec((None, 1, tf), lambda i, j: (layer, 0, j)),
                  pl.BlockSpec((None, 1, tf), lambda i, j: (layer, 0, nj + j))],
        out_specs=pl.BlockSpec((tm, tf), lambda i, j: (i, j)),
        out_shape=jax.ShapeDtypeStruct((N_TOK, D_FF), BF16),
        compiler_params=_cparams(("parallel", "arbitrary"), 48),
        name="ffn_up",
    )(h2, w_up, w_up, conv_w, conv_w, conv_b3, conv_b3)


def _rmsnorm_kernel(x_ref, g_ref, o_ref):
    x = x_ref[...]
    o_ref[...] = x * lax.rsqrt(jnp.mean(x * x, axis=-1, keepdims=True) + EPS) * g_ref[...]


def _final_norm(x, g, row0, rows):
    tm = 512
    return pl.pallas_call(
        _rmsnorm_kernel,
        grid=(rows // tm,),
        in_specs=[pl.BlockSpec((tm, D_MODEL), lambda i: (row0 // tm + i, 0)),
                  pl.BlockSpec((1, D_MODEL), lambda i: (0, 0))],
        out_specs=pl.BlockSpec((tm, D_MODEL), lambda i: (i, 0)),
        out_shape=jax.ShapeDtypeStruct((rows, D_MODEL), F32),
        compiler_params=_cparams(("parallel",), 32),
        name="final_norm",
    )(x, g.reshape(1, D_MODEL))


def _dot_nt(a, b):
    return lax.dot_general(a, b, (((1,), (1,)), ((), ())), preferred_element_type=F32)


def _lane_is_low(rows):
    return lax.broadcasted_iota(jnp.int32, (rows, LANES), 1) < HALF


def _one_head_query(q2, low, half, kv_half):
    qa = jnp.where(low if half == 0 else jnp.logical_not(low), q2, 0.0)
    if half != kv_half:
        qa = pltpu.roll(qa, HALF, 1)
    return qa.astype(BF16)


def _attend(qa, segments, sink=None):
    logits = []
    for k, _, fix in segments:
        s = _dot_nt(qa, k)
        logits.append(s if fix is None else fix(s))
    m = functools.reduce(jnp.maximum, [jnp.max(s, axis=-1, keepdims=True) for s in logits])
    if sink is not None:
        m = jnp.maximum(m, sink)
    es = [jnp.exp(s - m) for s in logits]
    den = functools.reduce(jnp.add, [jnp.sum(e, axis=-1, keepdims=True) for e in es])
    if sink is not None:
        den = den + jnp.exp(sink - m)
    o = functools.reduce(jnp.add, [jnp.dot(e.astype(BF16), v, preferred_element_type=F32)
                                   for e, (_, v, _) in zip(es, segments)])
    return o / den


def _gqa_heads(q_tile, segments_of, sink_ref, layer, low, rows):
    head_out = {}
    for kv in range(WIN_KV_HEADS):
        kt, kv_half = kv // 2, kv % 2
        heads = [kv * WIN_GROUP + g for g in range(WIN_GROUP)]
        qa = jnp.concatenate([_one_head_query(q_tile(h // 2), low, h % 2, kv_half) for h in heads], axis=0)
        sink = jnp.concatenate([jnp.full((rows, 1), sink_ref[layer, h], F32) for h in heads], axis=0)
        o = _attend(qa, segments_of(kt), sink)
        for g, h in enumerate(heads):
            og = o[g * rows:(g + 1) * rows]
            head_out[h] = og if h % 2 == kv_half else pltpu.roll(og, HALF, 1)
    return jnp.concatenate([jnp.where(low, head_out[2 * j], head_out[2 * j + 1]).astype(BF16)
                            for j in range(WIN_HEADS // 2)], axis=1)


def _pair_heads(q2, segments, low, rows):
    qa = jnp.concatenate([_one_head_query(q2, low, 0, 0), _one_head_query(q2, low, 1, 1)], axis=0)
    o = _attend(qa, segments)
    return jnp.where(low, o[:rows], o[rows:]).astype(BF16)


def _ctx_attn_kernel(sink_ref, qw_ref, kw_ref, vw_ref, qn_ref, kn_ref, vn_ref, ow_ref, on_ref, *, layer):
    low = _lane_is_low(SEQ)
    tile = lambda ref, j: ref[:, j * LANES:(j + 1) * LANES]

    for j in range(WIN_HEADS // 2):
        q2 = tile(qw_ref, j) * ATT_SCALE
        outs = []
        for half in range(2):
            head = 2 * j + half
            kv = head // WIN_GROUP
            kt, kv_half = kv // 2, kv % 2
            segment = [(tile(kw_ref, kt).astype(BF16), tile(vw_ref, kt).astype(BF16), None)]
            sink = jnp.full((SEQ, 1), sink_ref[layer, head], F32)
            o = _attend(_one_head_query(q2, low, half, kv_half), segment, sink)
            outs.append(o if half == kv_half else pltpu.roll(o, HALF, 1))
        ow_ref[:, j * LANES:(j + 1) * LANES] = jnp.where(low, outs[0], outs[1]).astype(BF16)

    for j in range(NA_HEADS // 2):
        q2 = tile(qn_ref, j) * ATT_SCALE
        segment = [(tile(kn_ref, j).astype(BF16), tile(vn_ref, j).astype(BF16), None)]
        outs = [_attend(_one_head_query(q2, low, half, half), segment) for half in range(2)]
        on_ref[:, j * LANES:(j + 1) * LANES] = jnp.where(low, outs[0], outs[1]).astype(BF16)


def _ctx_attention(proj, win_sink, layer):
    w768 = lambda off: pl.BlockSpec((SEQ, 768), lambda b: (b, off // 768))
    w256 = lambda off: pl.BlockSpec((SEQ, WIN_KV), lambda b: (b, off // WIN_KV))
    out_spec = pl.BlockSpec((SEQ, 768), lambda b: (b, 0))
    return pl.pallas_call(
        functools.partial(_ctx_attn_kernel, layer=layer),
        grid=(BATCH,),
        in_specs=[pl.BlockSpec(memory_space=pltpu.SMEM),
                  w768(QW_OFF), w256(KW_OFF), w256(VW_OFF), w768(QN_OFF), w768(KN_OFF), w768(VN_OFF)],
        out_specs=[out_spec, out_spec],
        out_shape=[jax.ShapeDtypeStruct((N_TOK, 768), BF16)] * 2,
        compiler_params=_cparams(("parallel",), 32),
        name="ctx_attention",
    )(win_sink, proj, proj, proj, proj, proj, proj)


def _rope_tables():
    nf = HEAD_DIM // 4
    pos = np.arange(DEC_SEQ)
    row = (pos // GRID_W).astype(np.float32)
    col = (pos % GRID_W).astype(np.float32)
    inv = (np.float32(ROPE_BASE) ** (-np.arange(nf, dtype=np.float32) / np.float32(nf))).astype(np.float32)
    ang_row = (row[:, None] * inv[None, :]).astype(np.float32)
    ang_col = (col[:, None] * inv[None, :]).astype(np.float32)
    cos_h = np.concatenate([np.cos(ang_row), np.cos(ang_row), np.cos(ang_col), np.cos(ang_col)], axis=-1)
    sin_h = np.concatenate([-np.sin(ang_row), np.sin(ang_row), -np.sin(ang_col), np.sin(ang_col)], axis=-1)
    cos_q = np.tile(cos_h, (1, WIN_HEADS)).astype(np.float32)
    sin_q = np.tile(sin_h, (1, WIN_HEADS)).astype(np.float32)
    return cos_q, sin_q


def _rope(x, cos, sin_signed):
    width = x.shape[-1]
    q = HEAD_DIM // 4
    lane = lax.broadcasted_iota(jnp.int32, x.shape, 1)
    partner = jnp.where((lane & (2 * q - 1)) < q, pltpu.roll(x, width - q, 1), pltpu.roll(x, q, 1))
    return x * cos + partner * sin_signed


def _win_attn_kernel(sink_ref, q_ref, kp_ref, kc_ref, kn_ref, vp_ref, vc_ref, vn_ref, ck_ref, cv_ref,
                     cosq_ref, sinq_ref, cosp_ref, sinp_ref, cosc_ref, sinc_ref, cosn_ref, sinn_ref,
                     o_prev_ref, o_ref, *, layer):
    del o_prev_ref
    n = pl.program_id(1)
    nb = DEC_SEQ // WIN_BLOCK
    low = _lane_is_low(WIN_BLOCK)

    q = _rope(q_ref[...], cosq_ref[...], sinq_ref[...]) * ATT_SCALE
    k_loc = jnp.concatenate([_rope(kp_ref[...], cosp_ref[...], sinp_ref[...]),
                             _rope(kc_ref[...], cosc_ref[...], sinc_ref[...]),
                             _rope(kn_ref[...], cosn_ref[...], sinn_ref[...])], axis=0).astype(BF16)
    v_loc = jnp.concatenate([vp_ref[...], vc_ref[...], vn_ref[...]], axis=0).astype(BF16)
    k_ctx = ck_ref[...].astype(BF16)
    v_ctx = cv_ref[...].astype(BF16)

    qi = lax.broadcasted_iota(jnp.int32, (WIN_BLOCK, 3 * WIN_BLOCK), 0)
    kj = lax.broadcasted_iota(jnp.int32, (WIN_BLOCK, 3 * WIN_BLOCK), 1)
    rel = kj - WIN_BLOCK - qi
    valid = jnp.logical_and(rel >= -WINDOW, rel <= WINDOW)
    valid = jnp.logical_and(valid, kj >= jnp.where(n > 0, 0, WIN_BLOCK))
    valid = jnp.logical_and(valid, kj < jnp.where(n < nb - 1, 3 * WIN_BLOCK, 2 * WIN_BLOCK))

    valid = jnp.concatenate([valid] * WIN_GROUP, axis=0)
    band = lambda s: jnp.where(valid, s, NEG_INF)

    def segments_of(kt):
        cols = slice(kt * LANES, (kt + 1) * LANES)
        return [(k_loc[:, cols], v_loc[:, cols], band), (k_ctx[:, cols], v_ctx[:, cols], None)]

    o_ref[...] = _gqa_heads(lambda j: q[:, j * LANES:(j + 1) * LANES], segments_of, sink_ref, layer, low, WIN_BLOCK)


def _win_attention(proj, cache_k, cache_v, win_sink, tables, o_ctx, layer):
    nb = DEC_SEQ // WIN_BLOCK
    row0 = N_CTX // WIN_BLOCK
    cos_q, sin_q, cos_k, sin_k = tables
    prev = lambda n: jnp.maximum(n - 1, 0)
    cur = lambda n: n
    nxt = lambda n: jnp.minimum(n + 1, nb - 1)
    kv_spec = lambda off, f: pl.BlockSpec((WIN_BLOCK, WIN_KV), lambda b, n: (row0 + b * nb + f(n), off // WIN_KV))
    tab_q = pl.BlockSpec((WIN_BLOCK, 768), lambda b, n: (n, 0))
    tab_k = lambda f: pl.BlockSpec((WIN_BLOCK, WIN_KV), lambda b, n: (f(n), 0))
    cache_spec = pl.BlockSpec((None, None, PAST_LEN, WIN_KV), lambda b, n: (b, layer, 0, 0))
    return pl.pallas_call(
        functools.partial(_win_attn_kernel, layer=layer),
        grid=(DEC_BATCH, nb),
        in_specs=[pl.BlockSpec(memory_space=pltpu.SMEM),
                  pl.BlockSpec((WIN_BLOCK, 768), lambda b, n: (row0 + b * nb + n, QW_OFF // 768)),
                  kv_spec(KW_OFF, prev), kv_spec(KW_OFF, cur), kv_spec(KW_OFF, nxt),
                  kv_spec(VW_OFF, prev), kv_spec(VW_OFF, cur), kv_spec(VW_OFF, nxt),
                  cache_spec, cache_spec,
                  tab_q, tab_q, tab_k(prev), tab_k(prev), tab_k(cur), tab_k(cur), tab_k(nxt), tab_k(nxt),
                  pl.BlockSpec(memory_space=pl.ANY)],
        out_specs=pl.BlockSpec((WIN_BLOCK, 768), lambda b, n: (row0 + b * nb + n, 0)),
        out_shape=jax.ShapeDtypeStruct((N_TOK, 768), BF16),
        input_output_aliases={18: 0},
        compiler_params=_cparams(("parallel", "parallel"), 32),
        name="win_attention",
    )(win_sink, proj, proj, proj, proj, proj, proj, proj, cache_k, cache_v,
      cos_q, sin_q, cos_k, sin_k, cos_k, sin_k, cos_k, sin_k, o_ctx)


def _na_window_start(r0):
    return min(max(r0 - NA_ROWS // 2, 0), GRID_H - NA_WIN_ROWS)


def _na_bias_kernel(rpb_ref, o_ref):
    head = pl.program_id(0)
    base = head * (RPB_H * RPB_W)
    qc = lax.broadcasted_iota(jnp.int32, (GRID_W, LANES), 0)
    lane = lax.broadcasted_iota(jnp.int32, (GRID_W, LANES), 1)
    kc = lane & (GRID_W - 1)
    low = lane < HALF
    dc_idx = jnp.clip(kc - qc + NA_COLS - 1, 0, RPB_W - 1)
    q_start = jnp.clip(qc - NA_COLS // 2, 0, GRID_W - NA_COLS)
    col_valid = jnp.logical_and(kc >= q_start, kc < q_start + NA_COLS)
    neg = jnp.full((GRID_W, LANES), NEG_INF, F32)

    pair = {}
    for d in range(-NA_ROWS, NA_ROWS):
        acc = jnp.zeros((GRID_W, LANES), F32)
        for dc in range(RPB_W):
            s_lo = rpb_ref[base + (d + NA_ROWS - 1) * RPB_W + dc] if abs(d) < NA_ROWS else 0.0
            s_hi = rpb_ref[base + (d + NA_ROWS) * RPB_W + dc] if abs(d + 1) < NA_ROWS else 0.0
            acc = jnp.where(dc_idx == dc, jnp.where(low, s_lo, s_hi), acc)
        pair[d] = jnp.where(col_valid, acc, NEG_INF)

    for qb in range(NA_QBLOCKS):
        r0 = NA_QROWS * qb
        ws = _na_window_start(r0)
        for qi in range(NA_QROWS):
            qr = r0 + qi
            rs = min(max(qr - NA_ROWS // 2, 0), GRID_H - NA_ROWS)
            for p in range(NA_WIN_ROWS // 2):
                kr = ws + 2 * p
                ok_lo = rs <= kr < rs + NA_ROWS
                ok_hi = rs <= kr + 1 < rs + NA_ROWS
                d = kr - qr
                if not (ok_lo or ok_hi):
                    tile = neg
                else:
                    tile = pair[d]
                    if not ok_lo:
                        tile = jnp.where(low, NEG_INF, tile)
                    if not ok_hi:
                        tile = jnp.where(low, tile, NEG_INF)
                o_ref[qb, qi * GRID_W:(qi + 1) * GRID_W, p * LANES:(p + 1) * LANES] = tile


def _na_bias(rpb_l):
    return pl.pallas_call(
        _na_bias_kernel,
        grid=(NA_HEADS,),
        in_specs=[pl.BlockSpec(memory_space=pltpu.SMEM)],
        out_specs=pl.BlockSpec((None, NA_QBLOCKS, NA_QROWS * GRID_W, NA_WIN_KEYS), lambda h: (h, 0, 0, 0)),
        out_shape=jax.ShapeDtypeStruct((NA_HEADS, NA_QBLOCKS, NA_QROWS * GRID_W, NA_WIN_KEYS), F32),
        compiler_params=_cparams(("parallel",), 32),
        name="na_bias",
    )(rpb_l.reshape(NA_HEADS * RPB_H * RPB_W))


def _na_attn_kernel(q_ref, k_ref, v_ref, ck_ref, cv_ref, bias_ref, o_prev_ref, o_ref):
    del o_prev_ref
    nq = NA_QROWS * GRID_W
    low = _lane_is_low(nq)
    k_all = k_ref[...].astype(BF16)
    v_all = v_ref[...].astype(BF16)
    k_ctx = ck_ref[...].astype(BF16)
    v_ctx = cv_ref[...].astype(BF16)
    blocks = []
    for qb in range(NA_QBLOCKS):
        start = _na_window_start(NA_QROWS * qb) * GRID_W
        k_loc = k_all[start:start + NA_WIN_KEYS]
        v_loc = v_all[start:start + NA_WIN_KEYS]
        bias = jnp.concatenate([bias_ref[0, qb], bias_ref[1, qb]], axis=0)
        q2 = q_ref[qb * nq:(qb + 1) * nq, :] * ATT_SCALE
        blocks.append(_pair_heads(q2, [(k_loc, v_loc, lambda s, bias=bias: s + bias), (k_ctx, v_ctx, None)], low, nq))
    o_ref[...] = jnp.concatenate(blocks, axis=0)


def _na_attention(proj, cache_k, cache_v, bias, o_ctx, layer):
    nq = NA_QROWS * GRID_W
    seq_blk0 = N_CTX // DEC_SEQ
    seq_spec = lambda off: pl.BlockSpec((DEC_SEQ, LANES), lambda hp, b: (seq_blk0 + b, off // LANES + hp))
    cache_spec = pl.BlockSpec((None, None, PAST_LEN, LANES), lambda hp, b: (b, layer, 0, hp))
    return pl.pallas_call(
        _na_attn_kernel,
        grid=(NA_HEADS // 2, DEC_BATCH),
        in_specs=[seq_spec(QN_OFF), seq_spec(KN_OFF), seq_spec(VN_OFF), cache_spec, cache_spec,
                  pl.BlockSpec((2, NA_QBLOCKS, nq, NA_WIN_KEYS), lambda hp, b: (hp, 0, 0, 0)),
                  pl.BlockSpec(memory_space=pl.ANY)],
        out_specs=pl.BlockSpec((DEC_SEQ, LANES), lambda hp, b: (seq_blk0 + b, hp)),
        out_shape=jax.ShapeDtypeStruct((N_TOK, 768), BF16),
        input_output_aliases={6: 0},
        compiler_params=_cparams(("parallel", "parallel"), 40),
        name="na_attention",
    )(proj, proj, proj, cache_k, cache_v, bias, o_ctx)


def _ssm_prep_kernel(lr_ref, li_ref, ldt_ref, br_ref, bi_ref, lbr_ref, lbi_ref, bbr_ref, bbi_ref):
    lr, li = lr_ref[...], li_ref[...]
    dt = jnp.exp(ldt_ref[...])
    mag = jnp.exp(lr * dt)
    ang = li * dt
    lbr = mag * jnp.cos(ang)
    lbi = mag * jnp.sin(ang)
    nr = lbr - 1.0
    den = lr * lr + li * li
    cr = (nr * lr + lbi * li) / den
    ci = (lbi * lr - nr * li) / den
    br, bi = br_ref[...], bi_ref[...]
    lbr_ref[...] = lbr
    lbi_ref[...] = lbi
    bbr_ref[...] = cr * br - ci * bi
    bbi_ref[...] = cr * bi + ci * br


def _ssm_prep(lam_re, lam_im, log_dt, b_re, b_im):
    shape5 = (DEPTH, 2, SSM_GROUPS, SSM_GROUP, SSM_STATE)
    rows = DEPTH * 2 * SSM_GROUPS * SSM_GROUP
    bc = lambda t: jnp.broadcast_to(t, shape5).reshape(rows, SSM_STATE)
    args = (bc(lam_re[:, :, :, None, :]), bc(lam_im[:, :, :, None, :]), bc(log_dt[:, :, :, None, None]),
            jnp.swapaxes(b_re, -1, -2).reshape(rows, SSM_STATE), jnp.swapaxes(b_im, -1, -2).reshape(rows, SSM_STATE))
    outs = pl.pallas_call(
        _ssm_prep_kernel,
        out_shape=[jax.ShapeDtypeStruct((rows, SSM_STATE), F32)] * 4,
        compiler_params=_cparams(None, 32),
        name="ssm_prep",
    )(*args)
    lbr, lbi, bbr, bbi = [o.reshape(shape5) for o in outs]
    return lbr[:, :, :, 0], lbi[:, :, :, 0], bbr, bbi


def _scan_permutation():
    p = np.zeros((SCAN_ROWS, SCAN_ROWS), np.float32)
    for t in range(SCAN_T):
        for s in range(SUBLANES):
            src_t = t if s < SUBLANES // 2 else SCAN_T - 1 - t
            p[t * SUBLANES + s, s * SCAN_T + src_t] = 1.0
    return p


def _ssm_scan_kernel(*refs, aliased):
    uf_ref, ub_ref, h0_ref, lam_ref, perm_ref, permt_ref, b_ref, c_ref = refs[:8]
    yf_ref, yb_ref, fin_ref, st_ref, bu_ref, y_ref = refs[8 + (2 if aliased else 0):]
    half_rows = SCAN_ROWS // 2

    @pl.when(pl.program_id(1) == 0)
    def _():
        st_ref[...] = h0_ref[...]

    u_bm = jnp.concatenate([uf_ref[...].reshape(half_rows, SSM_WIDTH),
                            ub_ref[...].reshape(half_rows, SSM_WIDTH)], axis=0).astype(BF16)
    u_tm = jnp.dot(perm_ref[...], u_bm, preferred_element_type=F32).astype(BF16)

    fwd = (lax.broadcasted_iota(jnp.int32, (SCAN_ROWS, 1), 0) & (SUBLANES - 1)) < SUBLANES // 2
    zero = jnp.zeros((SCAN_ROWS, LANES), BF16)
    for kb in range(SSM_KB):
        ub = u_tm[:, kb * LANES:(kb + 1) * LANES]
        lhs = jnp.concatenate([jnp.where(fwd, ub, zero), jnp.where(fwd, zero, ub)], axis=1)
        bu_ref[...] = jnp.dot(lhs, b_ref[kb], preferred_element_type=F32)
        ar = lam_ref[kb, 0]
        ai = lam_ref[kb, 1]

        def step(t, carry):
            xr, xi = carry
            r = pl.multiple_of(t * SUBLANES, SUBLANES)
            nr = ar * xr - ai * xi + bu_ref[pl.ds(r, SUBLANES), 0:STATE_COLS]
            ni = ar * xi + ai * xr + bu_ref[pl.ds(r, SUBLANES), STATE_COLS:2 * STATE_COLS]
            bu_ref[pl.ds(r, SUBLANES), 0:STATE_COLS] = nr
            bu_ref[pl.ds(r, SUBLANES), STATE_COLS:2 * STATE_COLS] = ni
            return nr, ni

        xr, xi = lax.fori_loop(0, SCAN_T, step, (st_ref[kb, 0], st_ref[kb, 1]))
        st_ref[kb, 0] = xr
        st_ref[kb, 1] = xi
        xs = bu_ref[...].astype(BF16)
        yy = jnp.dot(xs, c_ref[kb], preferred_element_type=F32)
        y_ref[:, kb * LANES:(kb + 1) * LANES] = jnp.where(fwd, yy[:, :LANES], yy[:, LANES:])
    fin_ref[...] = st_ref[...]

    y = y_ref[...]
    hi = y.astype(BF16)
    rest = y - hi.astype(F32)
    mid = rest.astype(BF16)
    low = (rest - mid.astype(F32)).astype(BF16)
    permt = permt_ref[...]
    y_bm = (jnp.dot(permt, hi, preferred_element_type=F32) + jnp.dot(permt, mid, preferred_element_type=F32)
            + jnp.dot(permt, low, preferred_element_type=F32))
    yf_ref[...] = y_bm[:half_rows].reshape(SUBLANES // 2, SCAN_T, SSM_WIDTH)
    yb_ref[...] = y_bm[half_rows:].reshape(SUBLANES // 2, SCAN_T, SSM_WIDTH)


def _ssm_scan(proj, seq_len, first_seq, n_seq, h0, lam, bmat, cmat, y_prev=None):
    n_all = N_TOK // seq_len
    nc = seq_len // SCAN_T
    blk0 = first_seq // 4
    proj3 = proj.reshape(n_all, seq_len, N_PROJ)
    state_shape = (SSM_KB, 2, SUBLANES, STATE_COLS)
    whole = lambda shape: pl.BlockSpec(shape, lambda b, c: (0,) * len(shape))
    state_spec = pl.BlockSpec((None,) + state_shape, lambda b, c: (b, 0, 0, 0, 0))
    u_spec = lambda chunk: pl.BlockSpec((4, SCAN_T, SSM_WIDTH), lambda b, c: (blk0 + b, chunk(c), U_OFF // SSM_WIDTH))
    y_spec = lambda chunk: pl.BlockSpec((4, SCAN_T, SSM_WIDTH), lambda b, c: (blk0 + b, chunk(c), 0))
    ascending = lambda c: c
    descending = lambda c: nc - 1 - c
    perm = _scan_permutation()
    aliased = y_prev is not None
    extra_specs = [pl.BlockSpec(memory_space=pl.ANY)] * 2 if aliased else []
    extra_args = [t.reshape(n_all, seq_len, SSM_WIDTH) for t in y_prev] if aliased else []
    yf, yb, fin = pl.pallas_call(
        functools.partial(_ssm_scan_kernel, aliased=aliased),
        grid=(n_seq // 4, nc),
        in_specs=[u_spec(ascending), u_spec(descending), state_spec, whole(state_shape),
                  whole((SCAN_ROWS, SCAN_ROWS)), whole((SCAN_ROWS, SCAN_ROWS)),
                  whole((SSM_KB, 2 * LANES, 2 * STATE_COLS)), whole((SSM_KB, 2 * STATE_COLS, 2 * LANES))] + extra_specs,
        out_specs=[y_spec(ascending), y_spec(descending), state_spec],
        out_shape=[jax.ShapeDtypeStruct((n_all, seq_len, SSM_WIDTH), F32)] * 2
                  + [jax.ShapeDtypeStruct((n_seq // 4,) + state_shape, F32)],
        scratch_shapes=[pltpu.VMEM(state_shape, F32), pltpu.VMEM((SCAN_ROWS, 2 * STATE_COLS), F32),
                        pltpu.VMEM((SCAN_ROWS, SSM_WIDTH), F32)],
        input_output_aliases={8: 0, 9: 1} if aliased else {},
        compiler_params=_cparams(("parallel", "arbitrary"), 40),
        name="ssm_scan",
    )(proj3, proj3, h0, lam, jnp.asarray(perm, BF16), jnp.asarray(perm.T, BF16), bmat, cmat, *extra_args)
    return yf.reshape(N_TOK, SSM_WIDTH), yb.reshape(N_TOK, SSM_WIDTH), fin


def _ssm_scan_params(lbr, lbi, bbr, bbi, c_re, c_im):
    eye = jnp.eye(GROUPS_PER_KB, dtype=F32)

    def lam_tile(t):
        t = t.reshape(DEPTH, 2, SSM_KB, STATE_COLS).transpose(0, 2, 1, 3)
        return jnp.repeat(t, SUBLANES // 2, axis=2)

    def b_blocks(t):
        t = t.reshape(DEPTH, 2, SSM_KB, GROUPS_PER_KB, SSM_GROUP, SSM_STATE)
        return jnp.einsum('ldkgnp,gh->ldkgnhp', t, eye).reshape(DEPTH, 2, SSM_KB, LANES, STATE_COLS)

    def c_blocks(t):
        t = t.reshape(DEPTH, 2, SSM_KB, GROUPS_PER_KB, SSM_GROUP, SSM_STATE)
        return jnp.einsum('ldkgnp,gh->ldkgphn', t, eye).reshape(DEPTH, 2, SSM_KB, STATE_COLS, LANES)

    lam = jnp.stack([lam_tile(lbr), lam_tile(lbi)], axis=2)
    b = jnp.concatenate([b_blocks(bbr), b_blocks(bbi)], axis=-1)
    bmat = jnp.concatenate([b[:, 0], b[:, 1]], axis=-2).astype(BF16)
    c = jnp.concatenate([c_blocks(c_re), -c_blocks(c_im)], axis=-2)
    cmat = jnp.concatenate([c[:, 0], c[:, 1]], axis=-1).astype(BF16)
    return lam, bmat, cmat


def _ssm_glu_kernel(yf_ref, yb_ref, u_ref, d_ref, w_ref, b_ref, o_ref):
    y = yf_ref[...] + yb_ref[...] + d_ref[...] * u_ref[...]
    y = jax.nn.gelu(y)
    z = jnp.dot(y.astype(BF16), w_ref[...], preferred_element_type=F32) + b_ref[...]
    o_ref[...] = (y * jax.nn.sigmoid(z)).astype(BF16)


def _ssm_glu(yf, yb, proj, d_skip, w_glu, b_glu, layer):
    tm = 512
    row = pl.BlockSpec((tm, SSM_WIDTH), lambda i: (i, 0))
    vec = pl.BlockSpec((None, 1, SSM_WIDTH), lambda i: (layer, 0, 0))
    return pl.pallas_call(
        _ssm_glu_kernel,
        grid=(N_TOK // tm,),
        in_specs=[row, row, pl.BlockSpec((tm, SSM_WIDTH), lambda i: (i, U_OFF // SSM_WIDTH)), vec,
                  pl.BlockSpec((None, SSM_WIDTH, SSM_WIDTH), lambda i: (layer, 0, 0)), vec],
        out_specs=row,
        out_shape=jax.ShapeDtypeStruct((N_TOK, SSM_WIDTH), BF16),
        compiler_params=_cparams(("parallel",), 32),
        name="ssm_glu",
    )(yf, yb, proj, d_skip.reshape(DEPTH, 1, SSM_WIDTH), w_glu, b_glu.reshape(DEPTH, 1, SSM_WIDTH))


def _ssm_mixer(proj, lam, bmat, cmat, h0_lat, d_skip, w_glu, b_glu, layer):
    h0_ctx = jnp.zeros((BATCH // 4, SSM_KB, 2, SUBLANES, STATE_COLS), F32)
    yf, yb, fin = _ssm_scan(proj, SEQ, 0, BATCH, h0_ctx, lam, bmat, cmat)
    yf, yb, _ = _ssm_scan(proj, DEC_SEQ, N_CTX // DEC_SEQ, DEC_BATCH, h0_lat, lam, bmat, cmat, y_prev=(yf, yb))
    return _ssm_glu(yf, yb, proj, d_skip, w_glu, b_glu, layer), fin


def _state_to_tiles(s_re, s_im):
    def one(t):
        t = t.transpose(1, 0, 2, 3).reshape(SUBLANES, SSM_KB, STATE_COLS)
        return t.transpose(1, 0, 2)
    return jnp.stack([one(s_re), one(s_im)], axis=1)[None]


def _tiles_to_state(fin, part):
    t = fin[:, :, part].reshape(BATCH // 4, SSM_KB, 2, 4, GROUPS_PER_KB, SSM_STATE)
    return t.transpose(0, 3, 2, 1, 4, 5).reshape(BATCH, 2, SSM_GROUPS, SSM_STATE)


def kernel(x_prompt, x_sample, cache_win_k, cache_win_v, cache_na_k, cache_na_v, state_ssm_re, state_ssm_im,
           c, c_ctx, norm1_g, norm2_g, w_mod, b_mod, w_in, ssm_lam_re, ssm_lam_im, ssm_log_dt,
           ssm_b_re, ssm_b_im, ssm_c_re, ssm_c_im, ssm_d, w_glu, b_glu, win_sink, na_rpb,
           w_branch, w_out, w_up, conv_w, conv_b, w_down, final_g):
    x = jnp.concatenate([x_prompt.reshape(N_CTX, D_MODEL), x_sample.reshape(N_LAT, D_MODEL)], axis=0)

    w_in_b, w_glu_b, w_branch_b = w_in.astype(BF16), w_glu.astype(BF16), w_branch.astype(BF16)
    w_out_b, w_down_b = w_out.astype(BF16), w_down.astype(BF16)

    cond8 = jnp.concatenate([c_ctx[None], c, jnp.zeros((SUBLANES - 1 - DEC_BATCH, D_MODEL), F32)], axis=0)
    mods = _adaln(cond8, w_mod, b_mod)
    mods = mods.reshape(DEPTH, SUBLANES, 6, D_MODEL).transpose(0, 2, 1, 3)[:, :, :, None, :]

    lbr, lbi, bbr, bbi = _ssm_prep(ssm_lam_re, ssm_lam_im, ssm_log_dt, ssm_b_re, ssm_b_im)
    lam, bmat, cmat = _ssm_scan_params(lbr, lbi, bbr, bbi, ssm_c_re, ssm_c_im)

    cos_q, sin_q = _rope_tables()
    tables = (jnp.asarray(cos_q), jnp.asarray(sin_q), jnp.asarray(cos_q[:, :WIN_KV]), jnp.asarray(sin_q[:, :WIN_KV]))
    cwk = cache_win_k.reshape(DEC_BATCH, DEPTH, PAST_LEN, WIN_KV)
    cwv = cache_win_v.reshape(DEC_BATCH, DEPTH, PAST_LEN, WIN_KV)
    cnk = cache_na_k.reshape(DEC_BATCH, DEPTH, PAST_LEN, NA_HEADS * HEAD_DIM)
    cnv = cache_na_v.reshape(DEC_BATCH, DEPTH, PAST_LEN, NA_HEADS * HEAD_DIM)

    new_wk, new_wv, new_nk, new_nv, new_sre, new_sim = [], [], [], [], [], []
    for l in range(DEPTH):
        gates, proj = _in_proj(x, norm1_g, mods, w_in_b, l)

        h0_lat = _state_to_tiles(state_ssm_re[:, l], state_ssm_im[:, l])
        o_ssm, fin = _ssm_mixer(proj, lam[l], bmat[l], cmat[l], h0_lat, ssm_d, w_glu_b, b_glu, l)

        o_win, o_na = _ctx_attention(proj, win_sink, l)
        o_win = _win_attention(proj, cwk, cwv, win_sink, tables, o_win, l)
        o_na = _na_attention(proj, cnk, cnv, _na_bias(na_rpb[l]), o_na, l)

        merged = _branch_merge(o_ssm, o_win, o_na, w_branch_b, gates, l)
        x, h2 = _out_proj(merged, w_out_b, x, mods, norm2_g, l)
        act = _ffn_up(h2, w_up, conv_w, conv_b, l)
        x = _matmul_residual(act, w_down_b, x, mods, l, 5, tm=1024, tn=512)

        ctx = proj[:N_CTX]
        new_wk.append(ctx[:, KW_OFF:KW_OFF + WIN_KV].reshape(BATCH, SEQ, WIN_KV_HEADS, HEAD_DIM))
        new_wv.append(ctx[:, VW_OFF:VW_OFF + WIN_KV].reshape(BATCH, SEQ, WIN_KV_HEADS, HEAD_DIM))
        new_nk.append(ctx[:, KN_OFF:KN_OFF + 768].reshape(BATCH, SEQ, NA_HEADS, HEAD_DIM))
        new_nv.append(ctx[:, VN_OFF:VN_OFF + 768].reshape(BATCH, SEQ, NA_HEADS, HEAD_DIM))
        new_sre.append(_tiles_to_state(fin, 0))
        new_sim.append(_tiles_to_state(fin, 1))

    y_ctx = _final_norm(x, final_g, 0, N_CTX)
    y_lat = _final_norm(x, final_g, N_CTX, N_LAT)
    return (y_ctx.reshape(BATCH, SEQ, D_MODEL), y_lat.reshape(DEC_BATCH, DEC_SEQ, D_MODEL),
            jnp.stack(new_wk, axis=1), jnp.stack(new_wv, axis=1), jnp.stack(new_nk, axis=1),
            jnp.stack(new_nv, axis=1), jnp.stack(new_sre, axis=1), jnp.stack(new_sim, axis=1))
```

```python
import functools

import numpy as np
import jax
import jax.numpy as jnp
from jax import lax
from jax.experimental import pallas as pl
from jax.experimental.pallas import tpu as pltpu

F32 = jnp.float32
BF16 = jnp.bfloat16

D_MODEL = 2048
BATCH = 16
SEQ = 256
DEPTH = 4
DEC_BATCH = 4
DEC_SEQ = 1024
PAST_LEN = 512
GRID_W = 64
GRID_H = DEC_SEQ // GRID_W
HEAD_DIM = 64
SSM_WIDTH = 768
SSM_GROUP = 16
SSM_GROUPS = SSM_WIDTH // SSM_GROUP
SSM_STATE = 64
WIN_HEADS = 12
WIN_KV_HEADS = 4
WIN_GROUP = WIN_HEADS // WIN_KV_HEADS
WINDOW = 128
WIN_BLOCK = 128
NA_HEADS = 12
NA_ROWS = 8
NA_COLS = 16
BRANCH_W = 768
D_FF = 5632
CONV_W = 3
ROPE_BASE = 10000.0
EPS = 1e-6
NEG_INF = -1e30
ATT_SCALE = HEAD_DIM ** -0.5
WIN_KV = WIN_KV_HEADS * HEAD_DIM

N_CTX = BATCH * SEQ
N_LAT = DEC_BATCH * DEC_SEQ
N_TOK = N_CTX + N_LAT

LANES = 128
SUBLANES = 8
HALF = LANES // 2
MIB = 1024 * 1024

N_GATE = 3 * D_MODEL
U_OFF = 0
QW_OFF = U_OFF + 768
QN_OFF = QW_OFF + 768
KN_OFF = QN_OFF + 768
VN_OFF = KN_OFF + 768
KW_OFF = VN_OFF + 768
VW_OFF = KW_OFF + WIN_KV
IN_TN = 1536
IN_SUB = 256
IN_SUBS = IN_TN // IN_SUB
N_PROJ = -(-(VW_OFF + WIN_KV) // IN_TN) * IN_TN

SSM_KB = SSM_WIDTH // LANES
GROUPS_PER_KB = LANES // SSM_GROUP
STATE_COLS = GROUPS_PER_KB * SSM_STATE
SCAN_T = 64
SCAN_ROWS = SCAN_T * SUBLANES

NA_QROWS = 2
NA_QBLOCKS = GRID_H // NA_QROWS
NA_WIN_ROWS = 10
NA_WIN_KEYS = NA_WIN_ROWS * GRID_W
RPB_H = 2 * NA_ROWS - 1
RPB_W = 2 * NA_COLS - 1


def _cparams(dims, vmem_mib):
    return pltpu.CompilerParams(dimension_semantics=dims, vmem_limit_bytes=int(vmem_mib * MIB))


def _mod_row(i, tm):
    tiles_ctx = N_CTX // tm
    per_batch = DEC_SEQ // tm
    return jnp.where(i < tiles_ctx, 0, 1 + (i - tiles_ctx) // per_batch)


def _mod_spec(layer, which, tm, tn, col_of):
    return pl.BlockSpec((None, None, None, 1, tn),
                        lambda i, j: (layer, which, _mod_row(i, tm), 0, col_of(i, j)))


def _adaln_kernel(c_ref, w_ref, b_ref, o_ref):
    s = jax.nn.silu(c_ref[...]).astype(BF16)
    o_ref[...] = jnp.dot(s, w_ref[...].astype(BF16), preferred_element_type=F32) + b_ref[...]


def _adaln(cond8, w_mod, b_mod):
    tn = 1024
    n = 6 * D_MODEL
    return pl.pallas_call(
        _adaln_kernel,
        grid=(DEPTH, n // tn),
        in_specs=[pl.BlockSpec((SUBLANES, D_MODEL), lambda l, j: (0, 0)),
                  pl.BlockSpec((None, D_MODEL, tn), lambda l, j: (l, 0, j)),
                  pl.BlockSpec((None, 1, tn), lambda l, j: (l, 0, j))],
        out_specs=pl.BlockSpec((None, SUBLANES, tn), lambda l, j: (l, 0, j)),
        out_shape=jax.ShapeDtypeStruct((DEPTH, SUBLANES, n), F32),
        compiler_params=_cparams(("parallel", "parallel"), 40),
        name="adaln",
    )(cond8, w_mod, b_mod.reshape(DEPTH, 1, n))


def _norm_mod(x, g, sh, sc):
    y = x * lax.rsqrt(jnp.mean(x * x, axis=-1, keepdims=True) + EPS)
    y = y * g
    return y * (1.0 + sc) + sh


def _in_proj_source_block(g):
    n_gate, n_uq, n_na, n_kvw = (N_GATE // IN_SUB, 2 * 768 // IN_SUB, 3 * 768 // IN_SUB, 2 * WIN_KV // IN_SUB)
    src_gate, src_na, src_kvw = 4352 // IN_SUB, 2048 // IN_SUB, 1536 // IN_SUB
    b1, b2, b3 = n_gate + n_uq, n_gate + n_uq + n_na, n_gate + n_uq + n_na + n_kvw
    return jnp.where(g < n_gate, src_gate + g,
                     jnp.where(g < b1, g - n_gate,
                               jnp.where(g < b2, src_na + g - b1,
                                         jnp.where(g < b3, src_kvw + g - b2, 0))))


def _in_proj_kernel(*refs):
    x_ref, g_ref, sh_ref, sc_ref = refs[:4]
    w_refs = refs[4:4 + IN_SUBS]
    gate_ref, proj_ref, h_ref = refs[4 + IN_SUBS:]
    j = pl.program_id(1)

    @pl.when(j == 0)
    def _():
        h_ref[...] = _norm_mod(x_ref[...], g_ref[...], sh_ref[...], sc_ref[...]).astype(BF16)

    def sub_blocks():
        for r, w_ref in enumerate(w_refs):
            yield slice(r * IN_SUB, (r + 1) * IN_SUB), jnp.dot(h_ref[...], w_ref[...], preferred_element_type=F32)

    @pl.when(j < N_GATE // IN_TN)
    def _():
        for cols, acc in sub_blocks():
            gate_ref[:, cols] = (0.5 * jnp.tanh(0.5 * acc) + 0.5).astype(BF16)

    @pl.when(j >= N_GATE // IN_TN)
    def _():
        for cols, acc in sub_blocks():
            proj_ref[:, cols] = acc


def _in_proj(x, norm_g, mods, w_in_b, layer):
    tm = 1024
    n_gate_tiles = N_GATE // IN_TN
    zero = lambda i, j: 0
    w_spec = lambda r: pl.BlockSpec((None, D_MODEL, IN_SUB),
                                    lambda i, j: (layer, 0, _in_proj_source_block(IN_SUBS * j + r)))
    return pl.pallas_call(
        _in_proj_kernel,
        grid=(N_TOK // tm, (N_GATE + N_PROJ) // IN_TN),
        in_specs=[pl.BlockSpec((tm, D_MODEL), lambda i, j: (i, 0)),
                  pl.BlockSpec((None, 1, D_MODEL), lambda i, j: (layer, 0, 0)),
                  _mod_spec(layer, 0, tm, D_MODEL, zero),
                  _mod_spec(layer, 1, tm, D_MODEL, zero)] + [w_spec(r) for r in range(IN_SUBS)],
        out_specs=[pl.BlockSpec((tm, IN_TN), lambda i, j: (i, jnp.minimum(j, n_gate_tiles - 1))),
                   pl.BlockSpec((tm, IN_TN), lambda i, j: (i, jnp.maximum(j - n_gate_tiles, 0)))],
        out_shape=[jax.ShapeDtypeStruct((N_TOK, N_GATE), BF16), jax.ShapeDtypeStruct((N_TOK, N_PROJ), F32)],
        scratch_shapes=[pltpu.VMEM((tm, D_MODEL), BF16)],
        compiler_params=_cparams(("parallel", "arbitrary"), 60),
        name="in_proj",
    )(x, norm_g.reshape(DEPTH, 1, D_MODEL), mods, mods, *([w_in_b] * IN_SUBS))


def _matmul_residual_kernel(a_ref, w_ref, x_ref, g_ref, o_ref):
    acc = jnp.dot(a_ref[...], w_ref[...], preferred_element_type=F32)
    o_ref[...] = x_ref[...] + g_ref[...] * acc


def _matmul_residual(a, w, x, mods, layer, which, tm, tn):
    k = a.shape[-1]
    return pl.pallas_call(
        _matmul_residual_kernel,
        grid=(N_TOK // tm, D_MODEL // tn),
        in_specs=[pl.BlockSpec((tm, k), lambda i, j: (i, 0)),
                  pl.BlockSpec((None, k, tn), lambda i, j: (layer, 0, j)),
                  pl.BlockSpec((tm, tn), lambda i, j: (i, j)),
                  _mod_spec(layer, which, tm, tn, lambda i, j: j)],
        out_specs=pl.BlockSpec((tm, tn), lambda i, j: (i, j)),
        out_shape=jax.ShapeDtypeStruct((N_TOK, D_MODEL), F32),
        compiler_params=_cparams(("parallel", "arbitrary"), 52),
        name="matmul_residual",
    )(a, w, x, mods)


def _out_proj_kernel(m_ref, w_ref, x_ref, g1_ref, ng_ref, sh_ref, sc_ref, xo_ref, h_ref):
    x_new = x_ref[...] + g1_ref[...] * jnp.dot(m_ref[...], w_ref[...], preferred_element_type=F32)
    xo_ref[...] = x_new
    h_ref[...] = _norm_mod(x_new, ng_ref[...], sh_ref[...], sc_ref[...]).astype(BF16)


def _out_proj(merged, w_out, x, mods, norm_g, layer):
    tm = 512
    row = pl.BlockSpec((tm, D_MODEL), lambda i: (i, 0))
    mod = lambda which: pl.BlockSpec((None, None, None, 1, D_MODEL), lambda i: (layer, which, _mod_row(i, tm), 0, 0))
    return pl.pallas_call(
        _out_proj_kernel,
        grid=(N_TOK // tm,),
        in_specs=[row, pl.BlockSpec((None, D_MODEL, D_MODEL), lambda i: (layer, 0, 0)), row,
                  mod(2), pl.BlockSpec((None, 1, D_MODEL), lambda i: (layer, 0, 0)), mod(3), mod(4)],
        out_specs=[row, row],
        out_shape=[jax.ShapeDtypeStruct((N_TOK, D_MODEL), F32), jax.ShapeDtypeStruct((N_TOK, D_MODEL), BF16)],
        compiler_params=_cparams(("parallel",), 52),
        name="out_proj",
    )(merged, w_out, x, mods, norm_g.reshape(DEPTH, 1, D_MODEL), mods, mods)


def _branch_merge_kernel(o0_ref, o1_ref, o2_ref, w0_ref, w1_ref, w2_ref, g0_ref, g1_ref, g2_ref, m_ref):
    m = g0_ref[...].astype(F32) * jnp.dot(o0_ref[...], w0_ref[...], preferred_element_type=F32)
    m = m + g1_ref[...].astype(F32) * jnp.dot(o1_ref[...], w1_ref[...], preferred_element_type=F32)
    m = m + g2_ref[...].astype(F32) * jnp.dot(o2_ref[...], w2_ref[...], preferred_element_type=F32)
    m_ref[...] = m.astype(BF16)


def _branch_merge(o_ssm, o_win, o_na, w_branch, gates, layer):
    tm, tn = 1024, 512
    nj = D_MODEL // tn
    o_spec = pl.BlockSpec((tm, BRANCH_W), lambda i, j: (i, 0))
    w_spec = lambda k: pl.BlockSpec((None, None, BRANCH_W, tn), lambda i, j: (layer, k, 0, j))
    g_spec = lambda k: pl.BlockSpec((tm, tn), lambda i, j: (i, k * nj + j))
    return pl.pallas_call(
        _branch_merge_kernel,
        grid=(N_TOK // tm, nj),
        in_specs=[o_spec, o_spec, o_spec, w_spec(0), w_spec(1), w_spec(2), g_spec(0), g_spec(1), g_spec(2)],
        out_specs=pl.BlockSpec((tm, tn), lambda i, j: (i, j)),
        out_shape=jax.ShapeDtypeStruct((N_TOK, D_MODEL), BF16),
        compiler_params=_cparams(("parallel", "arbitrary"), 48),
        name="branch_merge",
    )(o_ssm, o_win, o_na, w_branch, w_branch, w_branch, gates, gates, gates)


FFN_TM = 1024
FFN_TF = 512


def _ffn_up_kernel(h_ref, wa_ref, wb_ref, cwa_ref, cwb_ref, cba_ref, cbb_ref, o_ref):
    i = pl.program_id(0)
    seq = jnp.where(i < N_CTX // FFN_TM, SEQ, DEC_SEQ)
    pos = lax.broadcasted_iota(jnp.int32, (FFN_TM, 1), 0) & (seq - 1)
    first = pos == 0
    last = pos == seq - 1

    def conv(u, cw_ref, cb_ref):
        prev = jnp.where(first, 0.0, pltpu.roll(u, 1, 0))
        nxt = jnp.where(last, 0.0, pltpu.roll(u, FFN_TM - 1, 0))
        return prev * cw_ref[0:1, :] + u * cw_ref[1:2, :] + nxt * cw_ref[2:3, :] + cb_ref[...]

    h = h_ref[...]
    a = conv(jnp.dot(h, wa_ref[...].astype(BF16), preferred_element_type=F32), cwa_ref, cba_ref)
    b = conv(jnp.dot(h, wb_ref[...].astype(BF16), preferred_element_type=F32), cwb_ref, cbb_ref)
    o_ref[...] = (jax.nn.silu(a) * b).astype(BF16)


def _ffn_up(h2, w_up, conv_w, conv_b, layer):
    tm, tf = FFN_TM, FFN_TF
    nj = D_FF // tf
    conv_b3 = conv_b.reshape(DEPTH, 1, 2 * D_FF)
    return pl.pallas_call(
        _ffn_up_kernel,
        grid=(N_TOK // tm, nj),
        in_specs=[pl.BlockSpec((tm, D_MODEL), lambda i, j: (i, 0)),
                  pl.BlockSpec((None, D_MODEL, tf), lambda i, j: (layer, 0, j)),
                  pl.BlockSpec((None, D_MODEL, tf), lambda i, j: (layer, 0, nj + j)),
                  pl.BlockSpec((None, CONV_W, tf), lambda i, j: (layer, 0, j)),
                  pl.BlockSpec((None, CONV_W, tf), lambda i, j: (layer, 0, nj + j)),
                  pl.BlockSpec((None, 1, tf), lambda i, j: (layer, 0, j)),
                  pl.BlockSpec((None, 1, tf), lambda i, j: (layer, 0, nj + j))],
        out_specs=pl.BlockSpec((tm, tf), lambda i, j: (i, j)),
        out_shape=jax.ShapeDtypeStruct((N_TOK, D_FF), BF16),
        compiler_params=_cparams(("parallel", "arbitrary"), 48),
        name="ffn_up",
    )(h2, w_up, w_up, conv_w, conv_w, conv_b3, conv_b3)


def _rmsnorm_kernel(x_ref, g_ref, o_ref):
    x = x_ref[...]
    o_ref[...] = x * lax.rsqrt(jnp.mean(x * x, axis=-1, keepdims=True) + EPS) * g_ref[...]


def _final_norm(x, g, row0, rows):
    tm = 512
    return pl.pallas_call(
        _rmsnorm_kernel,
        grid=(rows // tm,),
        in_specs=[pl.BlockSpec((tm, D_MODEL), lambda i: (row0 // tm + i, 0)),
                  pl.BlockSpec((1, D_MODEL), lambda i: (0, 0))],
        out_specs=pl.BlockSpec((tm, D_MODEL), lambda i: (i, 0)),
        out_shape=jax.ShapeDtypeStruct((rows, D_MODEL), F32),
        compiler_params=_cparams(("parallel",), 32),
        name="final_norm",
    )(x, g.reshape(1, D_MODEL))


def _dot_nt(a, b):
    return lax.dot_general(a, b, (((1,), (1,)), ((), ())), preferred_element_type=F32)


def _lane_is_low(rows):
    return lax.broadcasted_iota(jnp.int32, (rows, LANES), 1) < HALF


def _one_head_query(q2, low, half, kv_half):
    qa = jnp.where(low if half == 0 else jnp.logical_not(low), q2, 0.0)
    if half != kv_half:
        qa = pltpu.roll(qa, HALF, 1)
    return qa.astype(BF16)


def _attend(qa, segments, sink=None):
    logits = []
    for k, _, fix in segments:
        s = _dot_nt(qa, k)
        logits.append(s if fix is None else fix(s))
    m = functools.reduce(jnp.maximum, [jnp.max(s, axis=-1, keepdims=True) for s in logits])
    if sink is not None:
        m = jnp.maximum(m, sink)
    es = [jnp.exp(s - m) for s in logits]
    den = functools.reduce(jnp.add, [jnp.sum(e, axis=-1, keepdims=True) for e in es])
    if sink is not None:
        den = den + jnp.exp(sink - m)
    o = functools.reduce(jnp.add, [jnp.dot(e.astype(BF16), v, preferred_element_type=F32)
                                   for e, (_, v, _) in zip(es, segments)])
    return o / den


def _gqa_heads(q_tile, segments_of, sink_ref, layer, low, rows):
    head_out = {}
    for kv in range(WIN_KV_HEADS):
        kt, kv_half = kv // 2, kv % 2
        heads = [kv * WIN_GROUP + g for g in range(WIN_GROUP)]
        qa = jnp.concatenate([_one_head_query(q_tile(h // 2), low, h % 2, kv_half) for h in heads], axis=0)
        sink = jnp.concatenate([jnp.full((rows, 1), sink_ref[layer, h], F32) for h in heads], axis=0)
        o = _attend(qa, segments_of(kt), sink)
        for g, h in enumerate(heads):
            og = o[g * rows:(g + 1) * rows]
            head_out[h] = og if h % 2 == kv_half else pltpu.roll(og, HALF, 1)
    return jnp.concatenate([jnp.where(low, head_out[2 * j], head_out[2 * j + 1]).astype(BF16)
                            for j in range(WIN_HEADS // 2)], axis=1)


def _pair_heads(q2, segments, low, rows):
    qa = jnp.concatenate([_one_head_query(q2, low, 0, 0), _one_head_query(q2, low, 1, 1)], axis=0)
    o = _attend(qa, segments)
    return jnp.where(low, o[:rows], o[rows:]).astype(BF16)


def _ctx_attn_kernel(sink_ref, qw_ref, kw_ref, vw_ref, qn_ref, kn_ref, vn_ref, ow_ref, on_ref, *, layer):
    low = _lane_is_low(SEQ)
    tile = lambda ref, j: ref[:, j * LANES:(j + 1) * LANES]

    for j in range(WIN_HEADS // 2):
        q2 = tile(qw_ref, j) * ATT_SCALE
        outs = []
        for half in range(2):
            head = 2 * j + half
            kv = head // WIN_GROUP
            kt, kv_half = kv // 2, kv % 2
            segment = [(tile(kw_ref, kt).astype(BF16), tile(vw_ref, kt).astype(BF16), None)]
            sink = jnp.full((SEQ, 1), sink_ref[layer, head], F32)
            o = _attend(_one_head_query(q2, low, half, kv_half), segment, sink)
            outs.append(o if half == kv_half else pltpu.roll(o, HALF, 1))
        ow_ref[:, j * LANES:(j + 1) * LANES] = jnp.where(low, outs[0], outs[1]).astype(BF16)

    for j in range(NA_HEADS // 2):
        q2 = tile(qn_ref, j) * ATT_SCALE
        segment = [(tile(kn_ref, j).astype(BF16), tile(vn_ref, j).astype(BF16), None)]
        outs = [_attend(_one_head_query(q2, low, half, half), segment) for half in range(2)]
        on_ref[:, j * LANES:(j + 1) * LANES] = jnp.where(low, outs[0], outs[1]).astype(BF16)


def _ctx_attention(proj, win_sink, layer):
    w768 = lambda off: pl.BlockSpec((SEQ, 768), lambda b: (b, off // 768))
    w256 = lambda off: pl.BlockSpec((SEQ, WIN_KV), lambda b: (b, off // WIN_KV))
    out_spec = pl.BlockSpec((SEQ, 768), lambda b: (b, 0))
    return pl.pallas_call(
        functools.partial(_ctx_attn_kernel, layer=layer),
        grid=(BATCH,),
        in_specs=[pl.BlockSpec(memory_space=pltpu.SMEM),
                  w768(QW_OFF), w256(KW_OFF), w256(VW_OFF), w768(QN_OFF), w768(KN_OFF), w768(VN_OFF)],
        out_specs=[out_spec, out_spec],
        out_shape=[jax.ShapeDtypeStruct((N_TOK, 768), BF16)] * 2,
        compiler_params=_cparams(("parallel",), 32),
        name="ctx_attention",
    )(win_sink, proj, proj, proj, proj, proj, proj)


def _rope_tables():
    nf = HEAD_DIM // 4
    pos = np.arange(DEC_SEQ)
    row = (pos // GRID_W).astype(np.float32)
    col = (pos % GRID_W).astype(np.float32)
    inv = (np.float32(ROPE_BASE) ** (-np.arange(nf, dtype=np.float32) / np.float32(nf))).astype(np.float32)
    ang_row = (row[:, None] * inv[None, :]).astype(np.float32)
    ang_col = (col[:, None] * inv[None, :]).astype(np.float32)
    cos_h = np.concatenate([np.cos(ang_row), np.cos(ang_row), np.cos(ang_col), np.cos(ang_col)], axis=-1)
    sin_h = np.concatenate([-np.sin(ang_row), np.sin(ang_row), -np.sin(ang_col), np.sin(ang_col)], axis=-1)
    cos_q = np.tile(cos_h, (1, WIN_HEADS)).astype(np.float32)
    sin_q = np.tile(sin_h, (1, WIN_HEADS)).astype(np.float32)
    return cos_q, sin_q


def _rope(x, cos, sin_signed):
    width = x.shape[-1]
    q = HEAD_DIM // 4
    lane = lax.broadcasted_iota(jnp.int32, x.shape, 1)
    partner = jnp.where((lane & (2 * q - 1)) < q, pltpu.roll(x, width - q, 1), pltpu.roll(x, q, 1))
    return x * cos + partner * sin_signed


def _win_attn_kernel(sink_ref, q_ref, kp_ref, kc_ref, kn_ref, vp_ref, vc_ref, vn_ref, ck_ref, cv_ref,
                     cosq_ref, sinq_ref, cosp_ref, sinp_ref, cosc_ref, sinc_ref, cosn_ref, sinn_ref,
                     o_prev_ref, o_ref, *, layer):
    del o_prev_ref
    n = pl.program_id(1)
    nb = DEC_SEQ // WIN_BLOCK
    low = _lane_is_low(WIN_BLOCK)

    q = _rope(q_ref[...], cosq_ref[...], sinq_ref[...]) * ATT_SCALE
    k_loc = jnp.concatenate([_rope(kp_ref[...], cosp_ref[...], sinp_ref[...]),
                             _rope(kc_ref[...], cosc_ref[...], sinc_ref[...]),
                             _rope(kn_ref[...], cosn_ref[...], sinn_ref[...])], axis=0).astype(BF16)
    v_loc = jnp.concatenate([vp_ref[...], vc_ref[...], vn_ref[...]], axis=0).astype(BF16)
    k_ctx = ck_ref[...].astype(BF16)
    v_ctx = cv_ref[...].astype(BF16)

    qi = lax.broadcasted_iota(jnp.int32, (WIN_BLOCK, 3 * WIN_BLOCK), 0)
    kj = lax.broadcasted_iota(jnp.int32, (WIN_BLOCK, 3 * WIN_BLOCK), 1)
    rel = kj - WIN_BLOCK - qi
    valid = jnp.logical_and(rel >= -WINDOW, rel <= WINDOW)
    valid = jnp.logical_and(valid, kj >= jnp.where(n > 0, 0, WIN_BLOCK))
    valid = jnp.logical_and(valid, kj < jnp.where(n < nb - 1, 3 * WIN_BLOCK, 2 * WIN_BLOCK))

    valid = jnp.concatenate([valid] * WIN_GROUP, axis=0)
    band = lambda s: jnp.where(valid, s, NEG_INF)

    def segments_of(kt):
        cols = slice(kt * LANES, (kt + 1) * LANES)
        return [(k_loc[:, cols], v_loc[:, cols], band), (k_ctx[:, cols], v_ctx[:, cols], None)]

    o_ref[...] = _gqa_heads(lambda j: q[:, j * LANES:(j + 1) * LANES], segments_of, sink_ref, layer, low, WIN_BLOCK)


def _win_attention(proj, cache_k, cache_v, win_sink, tables, o_ctx, layer):
    nb = DEC_SEQ // WIN_BLOCK
    row0 = N_CTX // WIN_BLOCK
    cos_q, sin_q, cos_k, sin_k = tables
    prev = lambda n: jnp.maximum(n - 1, 0)
    cur = lambda n: n
    nxt = lambda n: jnp.minimum(n + 1, nb - 1)
    kv_spec = lambda off, f: pl.BlockSpec((WIN_BLOCK, WIN_KV), lambda b, n: (row0 + b * nb + f(n), off // WIN_KV))
    tab_q = pl.BlockSpec((WIN_BLOCK, 768), lambda b, n: (n, 0))
    tab_k = lambda f: pl.BlockSpec((WIN_BLOCK, WIN_KV), lambda b, n: (f(n), 0))
    cache_spec = pl.BlockSpec((None, None, PAST_LEN, WIN_KV), lambda b, n: (b, layer, 0, 0))
    return pl.pallas_call(
        functools.partial(_win_attn_kernel, layer=layer),
        grid=(DEC_BATCH, nb),
        in_specs=[pl.BlockSpec(memory_space=pltpu.SMEM),
                  pl.BlockSpec((WIN_BLOCK, 768), lambda b, n: (row0 + b * nb + n, QW_OFF // 768)),
                  kv_spec(KW_OFF, prev), kv_spec(KW_OFF, cur), kv_spec(KW_OFF, nxt),
                  kv_spec(VW_OFF, prev), kv_spec(VW_OFF, cur), kv_spec(VW_OFF, nxt),
                  cache_spec, cache_spec,
                  tab_q, tab_q, tab_k(prev), tab_k(prev), tab_k(cur), tab_k(cur), tab_k(nxt), tab_k(nxt),
                  pl.BlockSpec(memory_space=pl.ANY)],
        out_specs=pl.BlockSpec((WIN_BLOCK, 768), lambda b, n: (row0 + b * nb + n, 0)),
        out_shape=jax.ShapeDtypeStruct((N_TOK, 768), BF16),
        input_output_aliases={18: 0},
        compiler_params=_cparams(("parallel", "parallel"), 32),
        name="win_attention",
    )(win_sink, proj, proj, proj, proj, proj, proj, proj, cache_k, cache_v,
      cos_q, sin_q, cos_k, sin_k, cos_k, sin_k, cos_k, sin_k, o_ctx)


def _na_window_start(r0):
    return min(max(r0 - NA_ROWS // 2, 0), GRID_H - NA_WIN_ROWS)


def _na_bias_kernel(rpb_ref, o_ref):
    head = pl.program_id(0)
    base = head * (RPB_H * RPB_W)
    qc = lax.broadcasted_iota(jnp.int32, (GRID_W, LANES), 0)
    lane = lax.broadcasted_iota(jnp.int32, (GRID_W, LANES), 1)
    kc = lane & (GRID_W - 1)
    low = lane < HALF
    dc_idx = jnp.clip(kc - qc + NA_COLS - 1, 0, RPB_W - 1)
    q_start = jnp.clip(qc - NA_COLS // 2, 0, GRID_W - NA_COLS)
    col_valid = jnp.logical_and(kc >= q_start, kc < q_start + NA_COLS)
    neg = jnp.full((GRID_W, LANES), NEG_INF, F32)

    pair = {}
    for d in range(-NA_ROWS, NA_ROWS):
        acc = jnp.zeros((GRID_W, LANES), F32)
        for dc in range(RPB_W):
            s_lo = rpb_ref[base + (d + NA_ROWS - 1) * RPB_W + dc] if abs(d) < NA_ROWS else 0.0
            s_hi = rpb_ref[base + (d + NA_ROWS) * RPB_W + dc] if abs(d + 1) < NA_ROWS else 0.0
            acc = jnp.where(dc_idx == dc, jnp.where(low, s_lo, s_hi), acc)
        pair[d] = jnp.where(col_valid, acc, NEG_INF)

    for qb in range(NA_QBLOCKS):
        r0 = NA_QROWS * qb
        ws = _na_window_start(r0)
        for qi in range(NA_QROWS):
            qr = r0 + qi
            rs = min(max(qr - NA_ROWS // 2, 0), GRID_H - NA_ROWS)
            for p in range(NA_WIN_ROWS // 2):
                kr = ws + 2 * p
                ok_lo = rs <= kr < rs + NA_ROWS
                ok_hi = rs <= kr + 1 < rs + NA_ROWS
                d = kr - qr
                if not (ok_lo or ok_hi):
                    tile = neg
                else:
                    tile = pair[d]
                    if not ok_lo:
                        tile = jnp.where(low, NEG_INF, tile)
                    if not ok_hi:
                        tile = jnp.where(low, tile, NEG_INF)
                o_ref[qb, qi * GRID_W:(qi + 1) * GRID_W, p * LANES:(p + 1) * LANES] = tile


def _na_bias(rpb_l):
    return pl.pallas_call(
        _na_bias_kernel,
        grid=(NA_HEADS,),
        in_specs=[pl.BlockSpec(memory_space=pltpu.SMEM)],
        out_specs=pl.BlockSpec((None, NA_QBLOCKS, NA_QROWS * GRID_W, NA_WIN_KEYS), lambda h: (h, 0, 0, 0)),
        out_shape=jax.ShapeDtypeStruct((NA_HEADS, NA_QBLOCKS, NA_QROWS * GRID_W, NA_WIN_KEYS), F32),
        compiler_params=_cparams(("parallel",), 32),
        name="na_bias",
    )(rpb_l.reshape(NA_HEADS * RPB_H * RPB_W))


def _na_attn_kernel(q_ref, k_ref, v_ref, ck_ref, cv_ref, bias_ref, o_prev_ref, o_ref):
    del o_prev_ref
    nq = NA_QROWS * GRID_W
    low = _lane_is_low(nq)
    k_all = k_ref[...].astype(BF16)
    v_all = v_ref[...].astype(BF16)
    k_ctx = ck_ref[...].astype(BF16)
    v_ctx = cv_ref[...].astype(BF16)
    blocks = []
    for qb in range(NA_QBLOCKS):
        start = _na_window_start(NA_QROWS * qb) * GRID_W
        k_loc = k_all[start:start + NA_WIN_KEYS]
        v_loc = v_all[start:start + NA_WIN_KEYS]
        bias = jnp.concatenate([bias_ref[0, qb], bias_ref[1, qb]], axis=0)
        q2 = q_ref[qb * nq:(qb + 1) * nq, :] * ATT_SCALE
        blocks.append(_pair_heads(q2, [(k_loc, v_loc, lambda s, bias=bias: s + bias), (k_ctx, v_ctx, None)], low, nq))
    o_ref[...] = jnp.concatenate(blocks, axis=0)


def _na_attention(proj, cache_k, cache_v, bias, o_ctx, layer):
    nq = NA_QROWS * GRID_W
    seq_blk0 = N_CTX // DEC_SEQ
    seq_spec = lambda off: pl.BlockSpec((DEC_SEQ, LANES), lambda hp, b: (seq_blk0 + b, off // LANES + hp))
    cache_spec = pl.BlockSpec((None, None, PAST_LEN, LANES), lambda hp, b: (b, layer, 0, hp))
    return pl.pallas_call(
        _na_attn_kernel,
        grid=(NA_HEADS // 2, DEC_BATCH),
        in_specs=[seq_spec(QN_OFF), seq_spec(KN_OFF), seq_spec(VN_OFF), cache_spec, cache_spec,
                  pl.BlockSpec((2, NA_QBLOCKS, nq, NA_WIN_KEYS), lambda hp, b: (hp, 0, 0, 0)),
                  pl.BlockSpec(memory_space=pl.ANY)],
        out_specs=pl.BlockSpec((DEC_SEQ, LANES), lambda hp, b: (seq_blk0 + b, hp)),
        out_shape=jax.ShapeDtypeStruct((N_TOK, 768), BF16),
        input_output_aliases={6: 0},
        compiler_params=_cparams(("parallel", "parallel"), 40),
        name="na_attention",
    )(proj, proj, proj, cache_k, cache_v, bias, o_ctx)


def _ssm_prep_kernel(lr_ref, li_ref, ldt_ref, br_ref, bi_ref, lbr_ref, lbi_ref, bbr_ref, bbi_ref):
    lr, li = lr_ref[...], li_ref[...]
    dt = jnp.exp(ldt_ref[...])
    mag = jnp.exp(lr * dt)
    ang = li * dt
    lbr = mag * jnp.cos(ang)
    lbi = mag * jnp.sin(ang)
    nr = lbr - 1.0
    den = lr * lr + li * li
    cr = (nr * lr + lbi * li) / den
    ci = (lbi * lr - nr * li) / den
    br, bi = br_ref[...], bi_ref[...]
    lbr_ref[...] = lbr
    lbi_ref[...] = lbi
    bbr_ref[...] = cr * br - ci * bi
    bbi_ref[...] = cr * bi + ci * br


def _ssm_prep(lam_re, lam_im, log_dt, b_re, b_im):
    shape5 = (DEPTH, 2, SSM_GROUPS, SSM_GROUP, SSM_STATE)
    rows = DEPTH * 2 * SSM_GROUPS * SSM_GROUP
    bc = lambda t: jnp.broadcast_to(t, shape5).reshape(rows, SSM_STATE)
    args = (bc(lam_re[:, :, :, None, :]), bc(lam_im[:, :, :, None, :]), bc(log_dt[:, :, :, None, None]),
            jnp.swapaxes(b_re, -1, -2).reshape(rows, SSM_STATE), jnp.swapaxes(b_im, -1, -2).reshape(rows, SSM_STATE))
    outs = pl.pallas_call(
        _ssm_prep_kernel,
        out_shape=[jax.ShapeDtypeStruct((rows, SSM_STATE), F32)] * 4,
        compiler_params=_cparams(None, 32),
        name="ssm_prep",
    )(*args)
    lbr, lbi, bbr, bbi = [o.reshape(shape5) for o in outs]
    return lbr[:, :, :, 0], lbi[:, :, :, 0], bbr, bbi


def _scan_permutation():
    p = np.zeros((SCAN_ROWS, SCAN_ROWS), np.float32)
    for t in range(SCAN_T):
        for s in range(SUBLANES):
            src_t = t if s < SUBLANES // 2 else SCAN_T - 1 - t
            p[t * SUBLANES + s, s * SCAN_T + src_t] = 1.0
    return p


def _ssm_scan_kernel(*refs, aliased):
    uf_ref, ub_ref, h0_ref, lam_ref, perm_ref, permt_ref, b_ref, c_ref = refs[:8]
    yf_ref, yb_ref, fin_ref, st_ref, bu_ref, y_ref = refs[8 + (2 if aliased else 0):]
    half_rows = SCAN_ROWS // 2

    @pl.when(pl.program_id(1) == 0)
    def _():
        st_ref[...] = h0_ref[...]

    u_bm = jnp.concatenate([uf_ref[...].reshape(half_rows, SSM_WIDTH),
                            ub_ref[...].reshape(half_rows, SSM_WIDTH)], axis=0)
    u_t = jnp.swapaxes(u_bm.reshape(SUBLANES, SCAN_T, SSM_WIDTH), 0, 1)
    u_r = jnp.concatenate([u_t[SCAN_T - 1 - t:SCAN_T - t] for t in range(SCAN_T)], axis=0)
    slot_fwd = lax.broadcasted_iota(jnp.int32, (SCAN_T, SUBLANES, 1), 1) < SUBLANES // 2
    u_tm = jnp.where(slot_fwd, u_t, u_r).reshape(SCAN_ROWS, SSM_WIDTH).astype(BF16)

    fwd = (lax.broadcasted_iota(jnp.int32, (SCAN_ROWS, 1), 0) & (SUBLANES - 1)) < SUBLANES // 2
    zero = jnp.zeros((SCAN_ROWS, LANES), BF16)
    for kb in range(SSM_KB):
        ub = u_tm[:, kb * LANES:(kb + 1) * LANES]
        lhs = jnp.concatenate([jnp.where(fwd, ub, zero), jnp.where(fwd, zero, ub)], axis=1)
        bu_ref[...] = jnp.dot(lhs, b_ref[kb], preferred_element_type=F32)
        ar = lam_ref[kb, 0]
        ai = lam_ref[kb, 1]

        def step(t, carry):
            xr, xi = carry
            r = pl.multiple_of(t * SUBLANES, SUBLANES)
            nr = ar * xr - ai * xi + bu_ref[pl.ds(r, SUBLANES), 0:STATE_COLS]
            ni = ar * xi + ai * xr + bu_ref[pl.ds(r, SUBLANES), STATE_COLS:2 * STATE_COLS]
            bu_ref[pl.ds(r, SUBLANES), 0:STATE_COLS] = nr
            bu_ref[pl.ds(r, SUBLANES), STATE_COLS:2 * STATE_COLS] = ni
            return nr, ni

        xr, xi = lax.fori_loop(0, SCAN_T, step, (st_ref[kb, 0], st_ref[kb, 1]))
        st_ref[kb, 0] = xr
        st_ref[kb, 1] = xi
        xs = bu_ref[...].astype(BF16)
        yy = jnp.dot(xs, c_ref[kb], preferred_element_type=F32)
        y_ref[:, kb * LANES:(kb + 1) * LANES] = jnp.where(fwd, yy[:, :LANES], yy[:, LANES:])
    fin_ref[...] = st_ref[...]

    y_t = y_ref[...].reshape(SCAN_T, SUBLANES, SSM_WIDTH)
    y_r = jnp.concatenate([y_t[SCAN_T - 1 - t:SCAN_T - t] for t in range(SCAN_T)], axis=0)
    y_bm = jnp.swapaxes(jnp.where(slot_fwd, y_t, y_r), 0, 1).reshape(SCAN_ROWS, SSM_WIDTH)
    yf_ref[...] = y_bm[:half_rows].reshape(SUBLANES // 2, SCAN_T, SSM_WIDTH)
    yb_ref[...] = y_bm[half_rows:].reshape(SUBLANES // 2, SCAN_T, SSM_WIDTH)


def _ssm_scan(proj, seq_len, first_seq, n_seq, h0, lam, bmat, cmat, y_prev=None):
    n_all = N_TOK // seq_len
    nc = seq_len // SCAN_T
    blk0 = first_seq // 4
    proj3 = proj.reshape(n_all, seq_len, N_PROJ)
    state_shape = (SSM_KB, 2, SUBLANES, STATE_COLS)
    whole = lambda shape: pl.BlockSpec(shape, lambda b, c: (0,) * len(shape))
    state_spec = pl.BlockSpec((None,) + state_shape, lambda b, c: (b, 0, 0, 0, 0))
    u_spec = lambda chunk: pl.BlockSpec((4, SCAN_T, SSM_WIDTH), lambda b, c: (blk0 + b, chunk(c), U_OFF // SSM_WIDTH))
    y_spec = lambda chunk: pl.BlockSpec((4, SCAN_T, SSM_WIDTH), lambda b, c: (blk0 + b, chunk(c), 0))
    ascending = lambda c: c
    descending = lambda c: nc - 1 - c
    perm = _scan_permutation()
    aliased = y_prev is not None
    extra_specs = [pl.BlockSpec(memory_space=pl.ANY)] * 2 if aliased else []
    extra_args = [t.reshape(n_all, seq_len, SSM_WIDTH) for t in y_prev] if aliased else []
    yf, yb, fin = pl.pallas_call(
        functools.partial(_ssm_scan_kernel, aliased=aliased),
        grid=(n_seq // 4, nc),
        in_specs=[u_spec(ascending), u_spec(descending), state_spec, whole(state_shape),
                  whole((SCAN_ROWS, SCAN_ROWS)), whole((SCAN_ROWS, SCAN_ROWS)),
                  whole((SSM_KB, 2 * LANES, 2 * STATE_COLS)), whole((SSM_KB, 2 * STATE_COLS, 2 * LANES))] + extra_specs,
        out_specs=[y_spec(ascending), y_spec(descending), state_spec],
        out_shape=[jax.ShapeDtypeStruct((n_all, seq_len, SSM_WIDTH), F32)] * 2
                  + [jax.ShapeDtypeStruct((n_seq // 4,) + state_shape, F32)],
        scratch_shapes=[pltpu.VMEM(state_shape, F32), pltpu.VMEM((SCAN_ROWS, 2 * STATE_COLS), F32),
                        pltpu.VMEM((SCAN_ROWS, SSM_WIDTH), F32)],
        input_output_aliases={8: 0, 9: 1} if aliased else {},
        compiler_params=_cparams(("parallel", "arbitrary"), 40),
        name="ssm_scan",
    )(proj3, proj3, h0, lam, jnp.asarray(perm, BF16), jnp.asarray(perm.T, BF16), bmat, cmat, *extra_args)
    return yf.reshape(N_TOK, SSM_WIDTH), yb.reshape(N_TOK, SSM_WIDTH), fin


def _ssm_scan_params(lbr, lbi, bbr, bbi, c_re, c_im):
    eye = jnp.eye(GROUPS_PER_KB, dtype=F32)

    def lam_tile(t):
        t = t.reshape(DEPTH, 2, SSM_KB, STATE_COLS).transpose(0, 2, 1, 3)
        return jnp.repeat(t, SUBLANES // 2, axis=2)

    def b_blocks(t):
        t = t.reshape(DEPTH, 2, SSM_KB, GROUPS_PER_KB, SSM_GROUP, SSM_STATE)
        return jnp.einsum('ldkgnp,gh->ldkgnhp', t, eye).reshape(DEPTH, 2, SSM_KB, LANES, STATE_COLS)

    def c_blocks(t):
        t = t.reshape(DEPTH, 2, SSM_KB, GROUPS_PER_KB, SSM_GROUP, SSM_STATE)
        return jnp.einsum('ldkgnp,gh->ldkgphn', t, eye).reshape(DEPTH, 2, SSM_KB, STATE_COLS, LANES)

    lam = jnp.stack([lam_tile(lbr), lam_tile(lbi)], axis=2)
    b = jnp.concatenate([b_blocks(bbr), b_blocks(bbi)], axis=-1)
    bmat = jnp.concatenate([b[:, 0], b[:, 1]], axis=-2).astype(BF16)
    c = jnp.concatenate([c_blocks(c_re), -c_blocks(c_im)], axis=-2)
    cmat = jnp.concatenate([c[:, 0], c[:, 1]], axis=-1).astype(BF16)
    return lam, bmat, cmat


def _ssm_glu_kernel(yf_ref, yb_ref, u_ref, d_ref, w_ref, b_ref, o_ref):
    y = yf_ref[...] + yb_ref[...] + d_ref[...] * u_ref[...]
    y = jax.nn.gelu(y)
    z = jnp.dot(y.astype(BF16), w_ref[...], preferred_element_type=F32) + b_ref[...]
    o_ref[...] = (y * jax.nn.sigmoid(z)).astype(BF16)


def _ssm_glu(yf, yb, proj, d_skip, w_glu, b_glu, layer):
    tm = 512
    row = pl.BlockSpec((tm, SSM_WIDTH), lambda i: (i, 0))
    vec = pl.BlockSpec((None, 1, SSM_WIDTH), lambda i: (layer, 0, 0))
    return pl.pallas_call(
        _ssm_glu_kernel,
        grid=(N_TOK // tm,),
        in_specs=[row, row, pl.BlockSpec((tm, SSM_WIDTH), lambda i: (i, U_OFF // SSM_WIDTH)), vec,
                  pl.BlockSpec((None, SSM_WIDTH, SSM_WIDTH), lambda i: (layer, 0, 0)), vec],
        out_specs=row,
        out_shape=jax.ShapeDtypeStruct((N_TOK, SSM_WIDTH), BF16),
        compiler_params=_cparams(("parallel",), 32),
        name="ssm_glu",
    )(yf, yb, proj, d_skip.reshape(DEPTH, 1, SSM_WIDTH), w_glu, b_glu.reshape(DEPTH, 1, SSM_WIDTH))


def _ssm_mixer(proj, lam, bmat, cmat, h0_lat, d_skip, w_glu, b_glu, layer):
    h0_ctx = jnp.zeros((BATCH // 4, SSM_KB, 2, SUBLANES, STATE_COLS), F32)
    yf, yb, fin = _ssm_scan(proj, SEQ, 0, BATCH, h0_ctx, lam, bmat, cmat)
    yf, yb, _ = _ssm_scan(proj, DEC_SEQ, N_CTX // DEC_SEQ, DEC_BATCH, h0_lat, lam, bmat, cmat, y_prev=(yf, yb))
    return _ssm_glu(yf, yb, proj, d_skip, w_glu, b_glu, layer), fin


def _state_to_tiles(s_re, s_im):
    def one(t):
        t = t.transpose(1, 0, 2, 3).reshape(SUBLANES, SSM_KB, STATE_COLS)
        return t.transpose(1, 0, 2)
    return jnp.stack([one(s_re), one(s_im)], axis=1)[None]


def _tiles_to_state(fin, part):
    t = fin[:, :, part].reshape(BATCH // 4, SSM_KB, 2, 4, GROUPS_PER_KB, SSM_STATE)
    return t.transpose(0, 3, 2, 1, 4, 5).reshape(BATCH, 2, SSM_GROUPS, SSM_STATE)


def kernel(x_prompt, x_sample, cache_win_k, cache_win_v, cache_na_k, cache_na_v, state_ssm_re, state_ssm_im,
           c, c_ctx, norm1_g, norm2_g, w_mod, b_mod, w_in, ssm_lam_re, ssm_lam_im, ssm_log_dt,
           ssm_b_re, ssm_b_im, ssm_c_re, ssm_c_im, ssm_d, w_glu, b_glu, win_sink, na_rpb,
           w_branch, w_out, w_up, conv_w, conv_b, w_down, final_g):
    x = jnp.concatenate([x_prompt.reshape(N_CTX, D_MODEL), x_sample.reshape(N_LAT, D_MODEL)], axis=0)

    w_in_b, w_glu_b, w_branch_b = w_in.astype(BF16), w_glu.astype(BF16), w_branch.astype(BF16)
    w_out_b, w_down_b = w_out.astype(BF16), w_down.astype(BF16)

    cond8 = jnp.concatenate([c_ctx[None], c, jnp.zeros((SUBLANES - 1 - DEC_BATCH, D_MODEL), F32)], axis=0)
    mods = _adaln(cond8, w_mod, b_mod)
    mods = mods.reshape(DEPTH, SUBLANES, 6, D_MODEL).transpose(0, 2, 1, 3)[:, :, :, None, :]

    lbr, lbi, bbr, bbi = _ssm_prep(ssm_lam_re, ssm_lam_im, ssm_log_dt, ssm_b_re, ssm_b_im)
    lam, bmat, cmat = _ssm_scan_params(lbr, lbi, bbr, bbi, ssm_c_re, ssm_c_im)

    cos_q, sin_q = _rope_tables()
    tables = (jnp.asarray(cos_q), jnp.asarray(sin_q), jnp.asarray(cos_q[:, :WIN_KV]), jnp.asarray(sin_q[:, :WIN_KV]))
    cwk = cache_win_k.reshape(DEC_BATCH, DEPTH, PAST_LEN, WIN_KV)
    cwv = cache_win_v.reshape(DEC_BATCH, DEPTH, PAST_LEN, WIN_KV)
    cnk = cache_na_k.reshape(DEC_BATCH, DEPTH, PAST_LEN, NA_HEADS * HEAD_DIM)
    cnv = cache_na_v.reshape(DEC_BATCH, DEPTH, PAST_LEN, NA_HEADS * HEAD_DIM)

    new_wk, new_wv, new_nk, new_nv, new_sre, new_sim = [], [], [], [], [], []
    for l in range(DEPTH):
        gates, proj = _in_proj(x, norm1_g, mods, w_in_b, l)

        h0_lat = _state_to_tiles(state_ssm_re[:, l], state_ssm_im[:, l])
        o_ssm, fin = _ssm_mixer(proj, lam[l], bmat[l], cmat[l], h0_lat, ssm_d, w_glu_b, b_glu, l)

        o_win, o_na = _ctx_attention(proj, win_sink, l)
        o_win = _win_attention(proj, cwk, cwv, win_sink, tables, o_win, l)
        o_na = _na_attention(proj, cnk, cnv, _na_bias(na_rpb[l]), o_na, l)

        merged = _branch_merge(o_ssm, o_win, o_na, w_branch_b, gates, l)
        x, h2 = _out_proj(merged, w_out_b, x, mods, norm2_g, l)
        act = _ffn_up(h2, w_up, conv_w, conv_b, l)
        x = _matmul_residual(act, w_down_b, x, mods, l, 5, tm=1024, tn=512)

        ctx = proj[:N_CTX]
        new_wk.append(ctx[:, KW_OFF:KW_OFF + WIN_KV].reshape(BATCH, SEQ, WIN_KV_HEADS, HEAD_DIM))
        new_wv.append(ctx[:, VW_OFF:VW_OFF + WIN_KV].reshape(BATCH, SEQ, WIN_KV_HEADS, HEAD_DIM))
        new_nk.append(ctx[:, KN_OFF:KN_OFF + 768].reshape(BATCH, SEQ, NA_HEADS, HEAD_DIM))
        new_nv.append(ctx[:, VN_OFF:VN_OFF + 768].reshape(BATCH, SEQ, NA_HEADS, HEAD_DIM))
        new_sre.append(_tiles_to_state(fin, 0))
        new_sim.append(_tiles_to_state(fin, 1))

    y_ctx = _final_norm(x, final_g, 0, N_CTX)
    y_lat = _final_norm(x, final_g, N_CTX, N_LAT)
    return (y_ctx.reshape(BATCH, SEQ, D_MODEL), y_lat.reshape(DEC_BATCH, DEC_SEQ, D_MODEL),
            jnp.stack(new_wk, axis=1), jnp.stack(new_wv, axis=1), jnp.stack(new_nk, axis=1),
            jnp.stack(new_nv, axis=1), jnp.stack(new_sre, axis=1), jnp.stack(new_sim, axis=1))
```
